```python
import math
import jax
import jax.numpy as jnp
from jax import lax
import numpy as np

D_MODEL = 1024
BATCH = 8
SEQ = 2048
DEPTH = 4

CTX_LEN = 256
GRID_W = 64
N_EVEN = (DEPTH + 1) // 2
N_ODD = DEPTH // 2
CHUNK = 64
MLSTM_HEADS = 4
MLSTM_HD = 128
MLSTM_W = MLSTM_HEADS * MLSTM_HD
CONF_CH = D_MODEL - MLSTM_W
CONF_K = 31
GDN_HEADS = 8
GDN_HD = 128
GDN_W = GDN_HEADS * GDN_HD
SHORT_K = 5
FFN = 2816
FFN_K = 3
FORGET_BIAS = 3.0
EPS = 1e-6
EVEN_IN = 4 * MLSTM_W + 4 * MLSTM_HEADS + 2 * CONF_CH
ODD_IN = 4 * GDN_W + 4 * GDN_HEADS
F32 = jnp.float32

kernel_name = "hybrid_mlstm_conformer_gdn_dit"


def _rms(x):
    xf = x.astype(F32)
    return xf * lax.rsqrt(jnp.mean(xf * xf, axis=-1, keepdims=True) + EPS)


def rms_norm(x, g):
    return (_rms(x) * g.astype(F32)).astype(x.dtype)


def ada_rms_norm(x, g, shift, scale):
    y = _rms(x) * g.astype(F32) * (1.0 + scale.astype(F32)) + shift.astype(F32)
    return y.astype(x.dtype)


def layer_norm(x, g, b):
    xf = x.astype(F32)
    mu = jnp.mean(xf, axis=-1, keepdims=True)
    xc = xf - mu
    var = jnp.mean(xc * xc, axis=-1, keepdims=True)
    return (xc * lax.rsqrt(var + EPS) * g.astype(F32) + b.astype(F32)).astype(x.dtype)


def l2_normalize(x):
    return x * lax.rsqrt(jnp.sum(x * x, axis=-1, keepdims=True) + EPS)


def dwconv1d(x, w):
    pad = w.shape[0] // 2
    return lax.conv_general_dilated(x, w[:, None, :].astype(x.dtype), (1,), [(pad, pad)],
                                    dimension_numbers=("NWC", "WIO", "NWC"),
                                    feature_group_count=x.shape[-1])


def dwconv2d_grid(x, w):
    bsz, t, ch = x.shape
    rows = t // GRID_W
    xg = x.reshape(bsz, rows, GRID_W, ch)
    ph, pw = w.shape[0] // 2, w.shape[1] // 2
    y = lax.conv_general_dilated(xg, w[:, :, None, :].astype(x.dtype), (1, 1), [(ph, ph), (pw, pw)],
                                 dimension_numbers=("NHWC", "HWIO", "NHWC"),
                                 feature_group_count=ch)
    return y.reshape(bsz, t, ch)


def _heads(a, n_heads, hd):
    bsz, t = a.shape[:2]
    return a.reshape(bsz, t, n_heads, hd).transpose(0, 2, 1, 3).astype(F32)


def _chunks(a):
    return a.reshape(a.shape[:2] + (a.shape[2] // CHUNK, CHUNK) + a.shape[3:])


def _unchunk(a):
    return a.reshape(a.shape[:2] + (-1,) + a.shape[4:])


def mlstm_direction(args, state, need_out):
    q, k, v, li, lf = (_chunks(a) for a in args)
    b = jnp.cumsum(lf, axis=-1)
    b_last = b[..., -1]
    w_log = b_last[..., None] - b + li
    m_loc = jnp.max(w_log, axis=-1)
    wgt = jnp.exp(w_log - m_loc[..., None])
    c_loc = jnp.einsum("bhnl,bhnlk,bhnlv->bhnkv", wgt, k, v)
    n_loc = jnp.einsum("bhnl,bhnlk->bhnk", wgt, k)

    def step(carry, inp):
        c_st, n_st, m_st = carry
        c_l, n_l, m_l, b_l = inp
        m_new = jnp.maximum(b_l + m_st, m_l)
        a_old = jnp.exp(b_l + m_st - m_new)
        a_loc = jnp.exp(m_l - m_new)
        c_new = a_old[..., None, None] * c_st + a_loc[..., None, None] * c_l
        n_new = a_old[..., None] * n_st + a_loc[..., None] * n_l
        return (c_new, n_new, m_new), (c_st, n_st, m_st)

    xs = tuple(jnp.moveaxis(a, 2, 0) for a in (c_loc, n_loc, m_loc, b_last))
    final, starts = lax.scan(step, state, xs)
    if not need_out:
        return None, final
    c0, n0, m0 = (jnp.moveaxis(a, 0, 2) for a in starts)
    causal = jnp.tril(jnp.ones((CHUNK, CHUNK), bool))
    d_log = jnp.where(causal, b[..., :, None] - b[..., None, :] + li[..., None, :], -jnp.inf)
    inter_log = b + m0[..., None]
    m_t = jnp.maximum(inter_log, jnp.max(d_log, axis=-1))
    p = jnp.exp(d_log - m_t[..., None]) * jnp.einsum("bhntk,bhnsk->bhnts", q, k)
    a_inter = jnp.exp(inter_log - m_t)
    num = (a_inter[..., None] * jnp.einsum("bhntk,bhnkv->bhntv", q, c0)
           + jnp.einsum("bhnts,bhnsv->bhntv", p, v))
    den = a_inter * jnp.einsum("bhntk,bhnk->bhnt", q, n0) + jnp.sum(p, axis=-1)
    h = num / jnp.maximum(jnp.abs(den), jnp.exp(-m_t))[..., None]
    return _unchunk(h), final


def gdn_direction(args, state, need_out):
    q, k, v, g, beta = (_chunks(a) for a in args)
    dv = v.shape[-1]
    incl = jnp.tril(jnp.ones((CHUNK, CHUNK), bool))
    strict = jnp.tril(jnp.ones((CHUNK, CHUNK), bool), -1)
    gc = jnp.cumsum(g, axis=-1)
    decay = jnp.exp(jnp.where(incl, gc[..., :, None] - gc[..., None, :], -jnp.inf))
    a_mat = jnp.where(strict, beta[..., :, None] * jnp.einsum("bhntk,bhnsk->bhnts", k, k) * decay, 0.0)
    rhs = jnp.concatenate([beta[..., None] * v, (beta * jnp.exp(gc))[..., None] * k], axis=-1)
    sol = lax.linalg.triangular_solve(a_mat, rhs, left_side=True, lower=True, unit_diagonal=True)
    u, w = sol[..., :dv], sol[..., dv:]
    g_last = gc[..., -1]
    k_dec = jnp.exp(g_last[..., None] - gc)[..., None] * k

    def step(s, inp):
        u_c, w_c, kd_c, gl_c = inp
        v_new = u_c - jnp.einsum("bhlk,bhkv->bhlv", w_c, s)
        s_new = jnp.exp(gl_c)[..., None, None] * s + jnp.einsum("bhlk,bhlv->bhkv", kd_c, v_new)
        return s_new, (s, v_new)

    xs = tuple(jnp.moveaxis(a, 2, 0) for a in (u, w, k_dec, g_last))
    final, (s0, v_new) = lax.scan(step, state, xs)
    if not need_out:
        return None, final
    s0 = jnp.moveaxis(s0, 0, 2)
    v_new = jnp.moveaxis(v_new, 0, 2)
    qk = jnp.einsum("bhntk,bhnsk->bhnts", q, k) * decay
    o = (jnp.einsum("bhntk,bhnkv->bhntv", jnp.exp(gc)[..., None] * q, s0)
         + jnp.einsum("bhnts,bhnsv->bhntv", qk, v_new))
    return _unchunk(o), final


def _flip(args):
    return tuple(jnp.flip(a, axis=2) for a in args)


def bidirectional(dir_fn, ctx_f, ctx_b, lat_f, lat_b, init, need_ctx):
    yc_f, sc_f = dir_fn(ctx_f, init, need_ctx)
    yc_b, sc_b = dir_fn(_flip(ctx_b), init, need_ctx)
    yl_f, _ = dir_fn(lat_f, sc_f, True)
    yl_b, _ = dir_fn(_flip(lat_b), sc_b, True)
    y_lat = yl_f + jnp.flip(yl_b, axis=2)
    y_ctx = yc_f + jnp.flip(yc_b, axis=2) if need_ctx else None
    return y_ctx, y_lat


def conformer_conv(glu, dw, dw_b, ln_g, ln_b):
    a, gt = jnp.split(glu, 2, axis=-1)
    y = dwconv1d(a * jax.nn.sigmoid(gt), dw) + dw_b
    return jax.nn.silu(layer_norm(y, ln_g, ln_b))


def even_mixer(hc, hl, w_in, b_in, head_g, conf_dw, conf_dw_b, conf_ln_g, conf_ln_b, w_out, need_ctx):
    W, H = MLSTM_W, MLSTM_HEADS

    def project(h):
        p = h @ w_in + b_in
        bsz, t = p.shape[:2]
        q = _heads(p[..., 0:W], H, MLSTM_HD) * (MLSTM_HD ** -0.5)
        k = _heads(p[..., W:2 * W], H, MLSTM_HD)
        v = _heads(p[..., 2 * W:3 * W], H, MLSTM_HD)
        og = p[..., 3 * W:4 * W]
        gates = p[..., 4 * W:4 * W + 4 * H].astype(F32).reshape(bsz, t, 4, H).transpose(2, 0, 3, 1)
        glu = p[..., 4 * W + 4 * H:]
        fwd = (q, k, v, gates[0], jax.nn.log_sigmoid(gates[1]))
        bwd = (q, k, v, gates[2], jax.nn.log_sigmoid(gates[3]))
        return fwd, bwd, og, glu

    def finish(y, og, glu):
        yh = _rms(jnp.swapaxes(y, 1, 2))
        bsz, t = yh.shape[:2]
        m_out = yh.reshape(bsz, t, W) * head_g.astype(F32) * jax.nn.sigmoid(og.astype(F32))
        c_out = conformer_conv(glu, conf_dw, conf_dw_b, conf_ln_g, conf_ln_b)
        return jnp.concatenate([m_out.astype(og.dtype), c_out], axis=-1) @ w_out

    cf, cb, c_og, c_glu = project(hc)
    lf_, lb, l_og, l_glu = project(hl)
    bsz = hl.shape[0]
    init = (jnp.zeros((bsz, H, MLSTM_HD, MLSTM_HD), F32), jnp.zeros((bsz, H, MLSTM_HD), F32),
            jnp.zeros((bsz, H), F32))
    yc, yl = bidirectional(mlstm_direction, cf, cb, lf_, lb, init, need_ctx)
    out_l = finish(yl, l_og, l_glu)
    out_c = finish(yc, c_og, c_glu) if need_ctx else None
    return out_c, out_l


def odd_mixer(hc, hl, w_in, short_w, a_log, dt_bias, head_g, w_out, need_ctx):
    W, H = GDN_W, GDN_HEADS
    a_rate = jnp.exp(a_log.astype(F32))
    dtb = dt_bias.astype(F32)

    def project(h):
        p = h @ w_in
        bsz, t = p.shape[:2]
        qkv = jax.nn.silu(dwconv1d(p[..., :3 * W], short_w))
        q = l2_normalize(_heads(qkv[..., 0:W], H, GDN_HD)) * (GDN_HD ** -0.5)
        k = l2_normalize(_heads(qkv[..., W:2 * W], H, GDN_HD))
        v = _heads(qkv[..., 2 * W:3 * W], H, GDN_HD)
        z = p[..., 3 * W:4 * W]
        ab = p[..., 4 * W:].astype(F32).reshape(bsz, t, 4, H).transpose(2, 0, 3, 1)
        g_f = -a_rate[0][None, :, None] * jax.nn.softplus(ab[0] + dtb[0][None, :, None])
        g_b = -a_rate[1][None, :, None] * jax.nn.softplus(ab[2] + dtb[1][None, :, None])
        fwd = (q, k, v, g_f, jax.nn.sigmoid(ab[1]))
        bwd = (q, k, v, g_b, jax.nn.sigmoid(ab[3]))
        return fwd, bwd, z

    def finish(y, z):
        yh = _rms(jnp.swapaxes(y, 1, 2)) * head_g.astype(F32)
        bsz, t = yh.shape[:2]
        zz = jax.nn.silu(z.astype(F32)).reshape(bsz, t, H, GDN_HD)
        return (yh * zz).reshape(bsz, t, W).astype(z.dtype) @ w_out

    cf, cb, c_z = project(hc)
    lf_, lb, l_z = project(hl)
    init = jnp.zeros((hl.shape[0], H, GDN_HD, GDN_HD), F32)
    yc, yl = bidirectional(gdn_direction, cf, cb, lf_, lb, init, need_ctx)
    out_l = finish(yl, l_z)
    out_c = finish(yc, c_z) if need_ctx else None
    return out_c, out_l


def conv_ffn(h, w_up, dw, dw_b, w_down, on_grid):
    gate, val = jnp.split(h @ w_up, 2, axis=-1)
    gate = dwconv2d_grid(gate, dw) if on_grid else dwconv1d(gate, dw[FFN_K // 2])
    return (jax.nn.silu(gate + dw_b) * val) @ w_down


def setup_inputs(seed: int = 0) -> dict:
    key = jax.random.key(seed)
    ks = iter(jax.random.split(key, 40))
    D = D_MODEL

    def nrm(shape, s):
        return jax.random.normal(next(ks), shape, F32) * s

    x = nrm((BATCH, SEQ, D), 1.0)
    c = nrm((BATCH, D), 1.0)
    ctx = nrm((BATCH, CTX_LEN, D), 1.0)
    c_ctx = nrm((D,), 1.0)
    ada_w = nrm((DEPTH, D, 6 * D), 0.5 * D ** -0.5)
    ada_b = nrm((DEPTH, 6 * D), 0.02)
    norm1_g = 1.0 + nrm((DEPTH, D), 0.02)
    norm2_g = 1.0 + nrm((DEPTH, D), 0.02)
    e_w_in = nrm((N_EVEN, D, EVEN_IN), D ** -0.5)
    f0 = 4 * MLSTM_W + MLSTM_HEADS
    f1 = 4 * MLSTM_W + 3 * MLSTM_HEADS
    e_b_in = nrm((N_EVEN, EVEN_IN), 0.02)
    e_b_in = e_b_in.at[:, f0:f0 + MLSTM_HEADS].add(FORGET_BIAS).at[:, f1:f1 + MLSTM_HEADS].add(FORGET_BIAS)
    e_head_g = 1.0 + nrm((N_EVEN, MLSTM_W), 0.02)
    e_conf_dw = nrm((N_EVEN, CONF_K, CONF_CH), CONF_K ** -0.5)
    e_conf_dw_b = nrm((N_EVEN, CONF_CH), 0.02)
    e_conf_ln_g = 1.0 + nrm((N_EVEN, CONF_CH), 0.02)
    e_conf_ln_b = nrm((N_EVEN, CONF_CH), 0.02)
    e_w_out = nrm((N_EVEN, D, D), D ** -0.5)
    o_w_in = nrm((N_ODD, D, ODD_IN), D ** -0.5)
    o_short_w = nrm((N_ODD, SHORT_K, 3 * GDN_W), SHORT_K ** -0.5)
    o_a_log = jnp.log(jax.random.uniform(next(ks), (N_ODD, 2, GDN_HEADS), F32, 1.0, 16.0))
    dt = jnp.exp(jax.random.uniform(next(ks), (N_ODD, 2, GDN_HEADS), F32, math.log(1e-3), math.log(1e-1)))
    o_dt_bias = dt + jnp.log(-jnp.expm1(-dt))
    o_head_g = 1.0 + nrm((N_ODD, GDN_HD), 0.02)
    o_w_out = nrm((N_ODD, GDN_W, D), GDN_W ** -0.5)
    f_w_up = nrm((DEPTH, D, 2 * FFN), D ** -0.5)
    f_dw = nrm((DEPTH, FFN_K, FFN_K, FFN), 1.0 / FFN_K)
    f_dw_b = nrm((DEPTH, FFN), 0.02)
    f_w_down = nrm((DEPTH, FFN, D), FFN ** -0.5)
    final_g = 1.0 + nrm((D,), 0.02)
    return {"x": x, "c": c, "ctx": ctx, "c_ctx": c_ctx, "ada_w": ada_w, "ada_b": ada_b,
            "norm1_g": norm1_g, "norm2_g": norm2_g,
            "e_w_in": e_w_in, "e_b_in": e_b_in, "e_head_g": e_head_g, "e_conf_dw": e_conf_dw,
            "e_conf_dw_b": e_conf_dw_b, "e_conf_ln_g": e_conf_ln_g, "e_conf_ln_b": e_conf_ln_b,
            "e_w_out": e_w_out, "o_w_in": o_w_in, "o_short_w": o_short_w, "o_a_log": o_a_log,
            "o_dt_bias": o_dt_bias, "o_head_g": o_head_g, "o_w_out": o_w_out,
            "f_w_up": f_w_up, "f_dw": f_dw, "f_dw_b": f_dw_b, "f_w_down": f_w_down,
            "final_g": final_g}


def reference(x, c, ctx, c_ctx, ada_w, ada_b, norm1_g, norm2_g,
              e_w_in, e_b_in, e_head_g, e_conf_dw, e_conf_dw_b, e_conf_ln_g, e_conf_ln_b, e_w_out,
              o_w_in, o_short_w, o_a_log, o_dt_bias, o_head_g, o_w_out,
              f_w_up, f_dw, f_dw_b, f_w_down, final_g):
    silu_c = jax.nn.silu(c)
    silu_cc = jax.nn.silu(c_ctx)
    for layer in range(DEPTH):
        need_ctx = layer < DEPTH - 1
        mod_l = (silu_c @ ada_w[layer] + ada_b[layer])[:, None, :]
        mod_c = (silu_cc @ ada_w[layer] + ada_b[layer])[None, None, :]
        sh1_l, sc1_l, g1_l, sh2_l, sc2_l, g2_l = jnp.split(mod_l, 6, axis=-1)
        sh1_c, sc1_c, g1_c, sh2_c, sc2_c, g2_c = jnp.split(mod_c, 6, axis=-1)
        hl = ada_rms_norm(x, norm1_g[layer], sh1_l, sc1_l)
        hc = ada_rms_norm(ctx, norm1_g[layer], sh1_c, sc1_c)
        j = layer // 2
        if layer % 2 == 0:
            yc, yl = even_mixer(hc, hl, e_w_in[j], e_b_in[j], e_head_g[j], e_conf_dw[j], e_conf_dw_b[j],
                                e_conf_ln_g[j], e_conf_ln_b[j], e_w_out[j], need_ctx)
        else:
            yc, yl = odd_mixer(hc, hl, o_w_in[j], o_short_w[j], o_a_log[j], o_dt_bias[j], o_head_g[j],
                               o_w_out[j], need_ctx)
        x = x + g1_l * yl
        hl = ada_rms_norm(x, norm2_g[layer], sh2_l, sc2_l)
        x = x + g2_l * conv_ffn(hl, f_w_up[layer], f_dw[layer], f_dw_b[layer], f_w_down[layer], True)
        if need_ctx:
            ctx = ctx + g1_c * yc
            hc = ada_rms_norm(ctx, norm2_g[layer], sh2_c, sc2_c)
            ctx = ctx + g2_c * conv_ffn(hc, f_w_up[layer], f_dw[layer], f_dw_b[layer], f_w_down[layer], False)
    return rms_norm(x, final_g)
```

```python
import functools

import jax
import jax.numpy as jnp
from jax import lax
from jax.experimental import pallas as pl
from jax.experimental.pallas import tpu as pltpu

F32 = jnp.float32
BF16 = jnp.bfloat16
EPS = 1e-6

LANES = 128
CHUNK = LANES
TOKEN_TILE = 256
GRID_W = 64
MLSTM_HEADS = 4
GDN_HEADS = 8
CONF_K = 31
SHORT_K = 5
GATE_ROWS = 8
FFN_CW = 256
VMEM_LIMIT = 56 * 1024 * 1024


def _cparams(sem):
    return pltpu.CompilerParams(dimension_semantics=sem, vmem_limit_bytes=VMEM_LIMIT)


def _sigmoid(x):
    return 1.0 / (1.0 + jnp.exp(-x))


def _silu(x):
    return x * _sigmoid(x)


def _softplus(x):
    return jnp.maximum(x, 0.0) + jnp.log1p(jnp.exp(-jnp.abs(x)))


def _log_sigmoid(x):
    return jnp.minimum(x, 0.0) - jnp.log1p(jnp.exp(-jnp.abs(x)))


def _mm(a, b):
    return jnp.dot(a.astype(BF16), b.astype(BF16), preferred_element_type=F32)


def _mm_nt(a, b):
    return lax.dot_general(a.astype(BF16), b.astype(BF16), (((1,), (1,)), ((), ())),
                           preferred_element_type=F32)


def _mm_tn(a, b):
    return lax.dot_general(a.astype(BF16), b.astype(BF16), (((0,), (0,)), ((), ())),
                           preferred_element_type=F32)


def _split2(x):
    hi = x.astype(BF16)
    lo = (x - hi.astype(F32)).astype(BF16)
    return hi, lo


def _split3(x):
    hi = x.astype(BF16)
    r = x - hi.astype(F32)
    mid = r.astype(BF16)
    lo = (r - mid.astype(F32)).astype(BF16)
    return hi, mid, lo


def _mm_x01(x, m01):
    hi, mid, lo = _split3(x)
    dot = functools.partial(jnp.dot, preferred_element_type=F32)
    return dot(hi, m01) + dot(mid, m01) + dot(lo, m01)


def _mm3(a, b):
    ah, al = _split2(a)
    bh, bl = _split2(b)
    dot = functools.partial(jnp.dot, preferred_element_type=F32)
    return dot(ah, bh) + dot(ah, bl) + dot(al, bh)


def _chunk_masks(L):
    row = lax.broadcasted_iota(jnp.int32, (L, L), 0)
    col = lax.broadcasted_iota(jnp.int32, (L, L), 1)
    return row, col


def _ada_kernel(c_ref, w_ref, b_ref, o_ref):
    c = c_ref[...]
    o_ref[0] = _mm(_silu(c), w_ref[0]) + b_ref[0]


def _ada_mod(c_rows, ada_w, ada_b):
    depth, d, n = ada_w.shape
    tn = 1536
    assert n % tn == 0
    rows = c_rows.shape[0]
    return pl.pallas_call(
        _ada_kernel,
        out_shape=jax.ShapeDtypeStruct((depth, rows, n), F32),
        grid=(depth, n // tn),
        in_specs=[pl.BlockSpec((rows, d), lambda l, j: (0, 0)),
                  pl.BlockSpec((1, d, tn), lambda l, j: (l, 0, j)),
                  pl.BlockSpec((1, 1, tn), lambda l, j: (l, 0, j))],
        out_specs=pl.BlockSpec((1, rows, tn), lambda l, j: (l, 0, j)),
        compiler_params=_cparams(("arbitrary", "arbitrary")),
        name="ada_mod",
    )(c_rows, ada_w, ada_b.reshape(depth, 1, n))


def _proj_kernel(x_ref, mod_ref, g_ref, w_ref, b_ref, *rest, mod_idx, segs, with_rows, tn):
    d = x_ref.shape[-1]
    if with_rows:
        wr_ref, br_ref = rest[:2]
        outs = rest[2:]
    else:
        outs = rest
    x = x_ref[0]
    ms = jnp.mean(x * x, axis=-1, keepdims=True)
    r = x * lax.rsqrt(ms + EPS)
    mod = mod_ref[0, 0]
    shift = mod[:, mod_idx * d:(mod_idx + 1) * d]
    scale = mod[:, (mod_idx + 1) * d:(mod_idx + 2) * d]
    h = r * g_ref[...] * (1.0 + scale) + shift
    hb = h.astype(BF16)
    for (off, width), o_ref in zip(segs, outs):
        for n0 in range(0, width, tn):
            nw = min(tn, width - n0)
            acc = jnp.dot(hb, w_ref[:, off + n0:off + n0 + nw], preferred_element_type=F32)
            o_ref[0, :, n0:n0 + nw] = acc + b_ref[:, off + n0:off + n0 + nw]
    if with_rows:
        rows = lax.dot_general(wr_ref[...], hb, (((1,), (1,)), ((), ())), preferred_element_type=F32)
        outs[-1][0] = rows + br_ref[...]


def _proj(xs, mod, g, w, b, *, mod_idx, seg_widths, w_rows=None, b_rows=None, tile0=0):
    bsz, s, d = xs.shape
    tm = TOKEN_TILE
    n_tiles = s // tm - tile0
    segs, off = [], 0
    for wd in seg_widths:
        segs.append((off, wd))
        off += wd
    assert off == w.shape[1]
    with_rows = w_rows is not None
    in_specs = [pl.BlockSpec((1, tm, d), lambda i, j: (i, j + tile0, 0)),
                pl.BlockSpec((1, 1, 1, mod.shape[-1]), lambda i, j: (i, jnp.minimum(j + tile0, 1), 0, 0)),
                pl.BlockSpec((1, d), lambda i, j: (0, 0)),
                pl.BlockSpec(w.shape, lambda i, j: (0, 0)),
                pl.BlockSpec((1, w.shape[1]), lambda i, j: (0, 0))]
    args = [xs, mod, g.reshape(1, d), w, b.reshape(1, -1)]
    out_shape = [jax.ShapeDtypeStruct((bsz, s, wd), F32) for wd in seg_widths]
    out_specs = [pl.BlockSpec((1, tm, wd), lambda i, j: (i, j + tile0, 0)) for wd in seg_widths]
    if with_rows:
        nr = w_rows.shape[0]
        in_specs += [pl.BlockSpec(w_rows.shape, lambda i, j: (0, 0)),
                     pl.BlockSpec((nr, 1), lambda i, j: (0, 0))]
        args += [w_rows, b_rows.reshape(nr, 1)]
        out_shape.append(jax.ShapeDtypeStruct((bsz, nr, s), F32))
        out_specs.append(pl.BlockSpec((1, nr, tm), lambda i, j: (i, 0, j + tile0)))
    kern = functools.partial(_proj_kernel, mod_idx=mod_idx, segs=tuple(segs), with_rows=with_rows, tn=512)
    return pl.pallas_call(
        kern, out_shape=out_shape, grid=(bsz, n_tiles), in_specs=in_specs, out_specs=out_specs,
        compiler_params=_cparams(("arbitrary", "arbitrary")), name="norm_proj",
    )(*args)


def _bwd_chunk(i, n_ctx, n_all):
    return jnp.where(i < n_ctx, n_ctx - 1 - i, n_all - 1 - (i - n_ctx))


def _col_bcast(masked_rows, ones_b):
    return _mm_x01(masked_rows, ones_b)


def _mlstm_kernel(q_ref, k_ref, v_ref, g_ref, y_ref, c_scr, *, n_ctx, n_all):
    L = CHUNK
    row, col = _chunk_masks(L)
    incl = (col <= row, col >= row)
    csum_r = (incl[1].astype(BF16), incl[0].astype(BF16))
    eye = row == col
    ones_b = jnp.ones((L, LANES), BF16)
    scale = float(LANES) ** -0.5

    y_ref[...] = jnp.zeros(y_ref.shape, F32)
    c_scr[...] = jnp.zeros(c_scr.shape, F32)

    def chunk(c, d, n0, m0):
        r0 = pl.multiple_of(c * L, L)
        q = q_ref[0, pl.ds(r0, L), :] * scale
        k = k_ref[0, pl.ds(r0, L), :]
        v = v_ref[0, pl.ds(r0, L), :]
        gt = g_ref[0, :, pl.ds(r0, L)]
        lsr = _log_sigmoid(gt)
        li_r = gt[2 * d:2 * d + 1, :]
        lf_r = lsr[2 * d + 1:2 * d + 2, :]
        b_r = _mm_x01(lsr, csum_r[d])[2 * d + 1:2 * d + 2, :]
        b_c = _col_bcast(jnp.where(incl[d], lf_r, 0.0), ones_b)
        li_c = _col_bcast(jnp.where(eye, li_r, 0.0), ones_b)
        total = b_r[:, L - 1:L] if d == 0 else b_r[:, 0:1]
        c0 = c_scr[d]
        wlog_r = total - b_r + li_r
        m_loc = jnp.max(wlog_r, axis=1, keepdims=True)
        kw = k * jnp.exp(total - b_c + li_c - m_loc)
        c_loc = _mm_tn(kw, v)
        n_loc = jnp.sum(kw, axis=0, keepdims=True)
        m_new = jnp.maximum(total + m0, m_loc)
        a_old = jnp.exp(total + m0 - m_new)
        a_loc = jnp.exp(m_loc - m_new)
        c_scr[d] = a_old * c0 + a_loc * c_loc
        n_new = a_old * n0 + a_loc * n_loc
        dlog = jnp.where(incl[d], b_c - b_r + li_r, -jnp.inf)
        dmax = jnp.max(dlog, axis=1, keepdims=True)
        inter = b_c + m0
        m_t = jnp.maximum(inter, dmax)
        p = jnp.exp(dlog - m_t) * _mm_nt(q, k)
        a_int = jnp.exp(inter - m_t)
        num = a_int * _mm(q, c0) + _mm(p, v)
        den = a_int * jnp.sum(q * n0, axis=1, keepdims=True) + jnp.sum(p, axis=1, keepdims=True)
        h = num / jnp.maximum(jnp.abs(den), jnp.exp(-m_t))
        y_ref[0, pl.ds(r0, L), :] += h
        return n_new, m_new

    def body(i, carry):
        nf, mf, nb, mb = carry
        nf, mf = chunk(i, 0, nf, mf)
        nb, mb = chunk(_bwd_chunk(i, n_ctx, n_all), 1, nb, mb)
        return nf, mf, nb, mb

    zn = jnp.zeros((1, LANES), F32)
    zm = jnp.zeros((1, 1), F32)
    lax.fori_loop(0, n_all, body, (zn, zm, zn, zm))


def _mlstm(qkvo, gates_t, *, n_ctx, n_all):
    bsz, s, _ = qkvo.shape
    h = MLSTM_HEADS
    kern = functools.partial(_mlstm_kernel, n_ctx=n_ctx, n_all=n_all)
    blk = lambda off: pl.BlockSpec((1, s, LANES), lambda i, j: (i, 0, j + off))
    return pl.pallas_call(
        kern, out_shape=jax.ShapeDtypeStruct((bsz, s, h * LANES), F32), grid=(bsz, h),
        in_specs=[blk(0), blk(h), blk(2 * h),
                  pl.BlockSpec((1, GATE_ROWS, s), lambda i, j: (i, j, 0))],
        out_specs=pl.BlockSpec((1, s, LANES), lambda i, j: (i, 0, j)),
        scratch_shapes=[pltpu.VMEM((2, LANES, LANES), F32)],
        compiler_params=_cparams(("arbitrary", "arbitrary")), name="mlstm_scan",
    )(qkvo, qkvo, qkvo, gates_t)


def _gdn_kernel(q_ref, k_ref, v_ref, wq_ref, wk_ref, wv_ref, ab_ref, alog_ref, dtb_ref, y_ref,
                xp_scr, q_scr, k_scr, v_scr, s_scr, *, n_ctx, n_all):
    L = CHUNK
    sc = n_ctx * L
    s_len = n_all * L
    row, col = _chunk_masks(L)
    incl = (col <= row, col >= row)
    strict = (col < row, col > row)
    csum_r = (incl[1].astype(BF16), incl[0].astype(BF16))
    eye = row == col
    eye_f = jnp.where(eye, 1.0, 0.0)
    blk = tuple(jnp.right_shift(row, sh) == jnp.right_shift(col, sh) for sh in (4, 5, 6, 7))
    ones_b = jnp.ones((L, LANES), BF16)
    qscale = float(LANES) ** -0.5
    pad = 8

    zeros_pad = jnp.zeros((pad, LANES), F32)
    xp_scr[0:pad, :] = zeros_pad
    xp_scr[pad + sc:2 * pad + sc, :] = zeros_pad
    xp_scr[2 * pad + s_len:3 * pad + s_len, :] = zeros_pad
    half = SHORT_K // 2
    for src, w_ref, dst, norm in ((q_ref, wq_ref, q_scr, True), (k_ref, wk_ref, k_scr, True),
                                  (v_ref, wv_ref, v_scr, False)):
        xp_scr[pad:pad + sc, :] = src[0, 0:sc, :]
        xp_scr[2 * pad + sc:2 * pad + s_len, :] = src[0, sc:s_len, :]
        for c in range(n_all):
            base = c * L + (pad if c < n_ctx else 2 * pad)
            acc = jnp.zeros((L, LANES), F32)
            for j in range(SHORT_K):
                acc = acc + w_ref[j:j + 1, :] * xp_scr[base + j - half:base + j - half + L, :]
            a = _silu(acc)
            if norm:
                a = a * lax.rsqrt(jnp.sum(a * a, axis=-1, keepdims=True) + EPS)
            dst[c * L:(c + 1) * L, :] = a

    y_ref[...] = jnp.zeros(y_ref.shape, F32)
    s_scr[...] = jnp.zeros(s_scr.shape, F32)
    arate = jnp.exp(alog_ref[...])
    dtb = dtb_ref[...]

    def chunk(c, d):
        r0 = pl.multiple_of(c * L, L)
        q = q_scr[pl.ds(r0, L), :] * qscale
        k = k_scr[pl.ds(r0, L), :]
        v = v_scr[pl.ds(r0, L), :]
        ab = ab_ref[0, :, pl.ds(r0, L)]
        g8 = -arate * _softplus(ab + dtb)
        beta_r = _sigmoid(ab[2 * d + 1:2 * d + 2, :])
        g_r = g8[2 * d:2 * d + 1, :]
        gc_r = _mm_x01(g8, csum_r[d])[2 * d:2 * d + 1, :]
        gc_c = _col_bcast(jnp.where(incl[d], g_r, 0.0), ones_b)
        beta_c = _col_bcast(jnp.where(eye, beta_r, 0.0), ones_b)
        total = gc_r[:, L - 1:L] if d == 0 else gc_r[:, 0:1]
        dec = jnp.exp(jnp.where(incl[d], gc_c - gc_r, -jnp.inf))
        kk = _mm_nt(k, k)
        a_mat = jnp.where(strict[d], beta_c * kk * dec, 0.0)
        egc = jnp.exp(gc_c)
        x = jnp.concatenate([beta_c * v, (beta_c * egc) * k], axis=1)
        pw = -jnp.where(blk[0], a_mat, 0.0)
        t_inv = eye_f + pw
        for _ in range(3):
            pw = _mm3(pw, pw)
            t_inv = t_inv + _mm3(t_inv, pw)
        for lvl in range(1, len(blk)):
            a_off = jnp.where(jnp.logical_and(blk[lvl], jnp.logical_not(blk[lvl - 1])), a_mat, 0.0)
            t_inv = t_inv - _mm3(_mm3(t_inv, a_off), t_inv)
        x = _mm3(t_inv, x)
        u = x[:, :LANES]
        w = x[:, LANES:]
        s0 = s_scr[d]
        v_new = u - _mm(w, s0)
        kdec = jnp.exp(total - gc_c) * k
        s_scr[d] = jnp.exp(total) * s0 + _mm_tn(kdec, v_new)
        qk = _mm_nt(q, k) * dec
        o = _mm(egc * q, s0) + _mm(qk, v_new)
        y_ref[0, pl.ds(r0, L), :] += o

    def body(i, carry):
        chunk(i, 0)
        chunk(_bwd_chunk(i, n_ctx, n_all), 1)
        return carry

    lax.fori_loop(0, n_all, body, 0)


def _gdn(qkv, short_w, ab_t, alog_rows, dtb_rows, *, n_ctx, n_all):
    bsz, s, _ = qkv.shape
    h = GDN_HEADS
    kern = functools.partial(_gdn_kernel, n_ctx=n_ctx, n_all=n_all)
    blk = lambda off: pl.BlockSpec((1, s, LANES), lambda i, j: (i, 0, j + off))
    wblk = lambda off: pl.BlockSpec((SHORT_K, LANES), lambda i, j: (0, j + off))
    return pl.pallas_call(
        kern, out_shape=jax.ShapeDtypeStruct((bsz, s, h * LANES), F32), grid=(bsz, h),
        in_specs=[blk(0), blk(h), blk(2 * h), wblk(0), wblk(h), wblk(2 * h),
                  pl.BlockSpec((1, GATE_ROWS, s), lambda i, j: (i, j, 0)),
                  pl.BlockSpec((GATE_ROWS, 1), lambda i, j: (j, 0)),
                  pl.BlockSpec((GATE_ROWS, 1), lambda i, j: (j, 0))],
        out_specs=pl.BlockSpec((1, s, LANES), lambda i, j: (i, 0, j)),
        scratch_shapes=[pltpu.VMEM((s + 24, LANES), F32), pltpu.VMEM((s, LANES), F32),
                        pltpu.VMEM((s, LANES), F32), pltpu.VMEM((s, LANES), F32),
                        pltpu.VMEM((2, LANES, LANES), F32)],
        compiler_params=_cparams(("arbitrary", "arbitrary")), name="gdn_scan",
    )(qkv, qkv, qkv, short_w, short_w, short_w, ab_t, alog_rows, dtb_rows)


def _conformer_kernel(glu_ref, w_ref, b_ref, lg_ref, lb_ref, o_ref, u_scr, *, sc, s_len):
    ch = o_ref.shape[-1]
    pad = 16
    rt = 32
    half = CONF_K // 2
    zeros_pad = jnp.zeros((pad, ch), F32)
    u_scr[0:pad, :] = zeros_pad
    u_scr[pad + sc:2 * pad + sc, :] = zeros_pad
    u_scr[2 * pad + s_len:3 * pad + s_len, :] = zeros_pad

    def fill(i, carry):
        r0 = pl.multiple_of(i * rt, rt)
        off = jnp.where(r0 < sc, pad, 2 * pad)
        a = glu_ref[0, pl.ds(r0, rt), 0:ch]
        gt = glu_ref[0, pl.ds(r0, rt), ch:2 * ch]
        u_scr[pl.ds(pl.multiple_of(r0 + off, 8), rt), :] = a * _sigmoid(gt)
        return carry

    lax.fori_loop(0, s_len // rt, fill, 0)

    def conv(i, carry):
        r0 = pl.multiple_of(i * rt, rt)
        off = jnp.where(r0 < sc, pad, 2 * pad)
        win = u_scr[pl.ds(pl.multiple_of(r0 + off - pad, 8), rt + 2 * pad), :]
        nwin = rt + 2 * pad
        acc = jnp.zeros((rt, ch), F32)
        for res in range(8):
            shifted = win if res == 0 else pltpu.roll(win, nwin - res, axis=0)
            for a8 in range(0, nwin - rt + 1, 8):
                j = a8 + res - (pad - half)
                if 0 <= j < CONF_K:
                    acc = acc + w_ref[j:j + 1, :] * shifted[a8:a8 + rt, :]
        y = acc + b_ref[...]
        mu = jnp.mean(y, axis=-1, keepdims=True)
        yc = y - mu
        var = jnp.mean(yc * yc, axis=-1, keepdims=True)
        z = yc * lax.rsqrt(var + EPS) * lg_ref[...] + lb_ref[...]
        o_ref[0, pl.ds(r0, rt), :] = _silu(z)
        return carry

    lax.fori_loop(0, s_len // rt, conv, 0)


def _conformer(glu, dw, dw_b, ln_g, ln_b, *, sc):
    bsz, s, c2 = glu.shape
    ch = c2 // 2
    kern = functools.partial(_conformer_kernel, sc=sc, s_len=s)
    vec = lambda: pl.BlockSpec((1, ch), lambda i: (0, 0))
    return pl.pallas_call(
        kern, out_shape=jax.ShapeDtypeStruct((bsz, s, ch), F32), grid=(bsz,),
        in_specs=[pl.BlockSpec((1, s, c2), lambda i: (i, 0, 0)),
                  pl.BlockSpec((CONF_K, ch), lambda i: (0, 0)), vec(), vec(), vec()],
        out_specs=pl.BlockSpec((1, s, ch), lambda i: (i, 0, 0)),
        scratch_shapes=[pltpu.VMEM((s + 48, ch), F32)],
        compiler_params=_cparams(("arbitrary",)), name="conformer_conv",
    )(glu, dw, dw_b.reshape(1, ch), ln_g.reshape(1, ch), ln_b.reshape(1, ch))


def _finish_kernel(y_ref, gate_ref, *rest, n_heads, even, mod_idx):
    if even:
        c_ref, x_ref, mod_ref, hg_ref, w_ref, o_ref = rest
    else:
        x_ref, mod_ref, hg_ref, w_ref, o_ref = rest
    d = x_ref.shape[-1]
    y = y_ref[0]
    parts = []
    for h in range(n_heads):
        yh = y[:, h * LANES:(h + 1) * LANES]
        ms = jnp.mean(yh * yh, axis=-1, keepdims=True)
        parts.append(yh * lax.rsqrt(ms + EPS))
    yn = jnp.concatenate(parts, axis=1)
    gate = gate_ref[0]
    if even:
        m = yn * hg_ref[...] * _sigmoid(gate)
        cat = jnp.concatenate([m.astype(BF16), c_ref[0].astype(BF16)], axis=1)
    else:
        cat = ((yn * hg_ref[...]) * _silu(gate)).astype(BF16)
    out = jnp.dot(cat, w_ref[...], preferred_element_type=F32)
    g1 = mod_ref[0, 0][:, mod_idx * d:(mod_idx + 1) * d]
    o_ref[0] = x_ref[0] + g1 * out


def _finish(y, gate_src, gate_blk, c_out, xs, mod, head_g, w_out, *, n_heads, tile0=0):
    bsz, s, d = xs.shape
    tm = TOKEN_TILE
    wdt = n_heads * LANES
    even = c_out is not None
    tok = lambda width, blk=0: pl.BlockSpec((1, tm, width), lambda i, j: (i, j + tile0, blk))
    in_specs = [tok(wdt), tok(wdt, gate_blk)]
    args = [y, gate_src]
    if even:
        in_specs.append(tok(c_out.shape[-1]))
        args.append(c_out)
    in_specs += [tok(d),
                 pl.BlockSpec((1, 1, 1, mod.shape[-1]), lambda i, j: (i, jnp.minimum(j + tile0, 1), 0, 0)),
                 pl.BlockSpec((1, wdt), lambda i, j: (0, 0)),
                 pl.BlockSpec(w_out.shape, lambda i, j: (0, 0))]
    args += [xs, mod, head_g.reshape(1, wdt), w_out]
    kern = functools.partial(_finish_kernel, n_heads=n_heads, even=even, mod_idx=2)
    return pl.pallas_call(
        kern, out_shape=jax.ShapeDtypeStruct((bsz, s, d), F32), grid=(bsz, s // tm - tile0),
        in_specs=in_specs, out_specs=tok(d),
        compiler_params=_cparams(("arbitrary", "arbitrary")), name="mixer_out",
    )(*args)


def _ffn_down_kernel(gm_ref, gp_ref, gn_ref, val_ref, x_ref, mod_ref, dw_ref, dwb_ref, wd_ref, *rest,
                     tile0, n_tiles_all, final, mod_idx):
    if final:
        fg_ref, o_ref, ext_scr, act_scr = rest
    else:
        o_ref, ext_scr, act_scr = rest
    tm = TOKEN_TILE
    gw = GRID_W
    f = gm_ref.shape[-1]
    d = x_ref.shape[-1]
    j = pl.program_id(1) + tile0
    lat = jnp.where(j > 0, 1.0, 0.0)
    jrow = jnp.broadcast_to(j, (gw, 1))
    has_prev = jrow >= 2
    has_next = jnp.logical_and(jrow >= 1, jrow <= n_tiles_all - 2)
    top = 8
    main = top + gw
    ext_scr[0:top, :] = jnp.zeros((top, f), F32)
    ext_scr[main + tm + gw:main + tm + gw + top, :] = jnp.zeros((top, f), F32)
    ext_scr[top:main, :] = jnp.where(has_prev, gp_ref[0], 0.0)
    ext_scr[main:main + tm, :] = gm_ref[0]
    ext_scr[main + tm:main + tm + gw, :] = jnp.where(has_next, gn_ref[0], 0.0)

    cidx = lax.broadcasted_iota(jnp.int32, (gw, 1), 0)
    jv = jrow
    for rs in range(tm // gw):
        left_ok = jnp.logical_or(cidx != 0, jv == 0) if rs > 0 else cidx != 0
        right_ok = jnp.logical_or(cidx != gw - 1, jv == 0) if rs < tm // gw - 1 else cidx != gw - 1
        for n0 in range(0, f, FFN_CW):
            base = main + rs * gw
            acc = jnp.zeros((gw, FFN_CW), F32)
            for dr in range(3):
                for dc in range(3):
                    st = base + (dr - 1) * gw + (dc - 1)
                    tap = ext_scr[st:st + gw, n0:n0 + FFN_CW]
                    wv = dw_ref[dr * 3 + dc:dr * 3 + dc + 1, n0:n0 + FFN_CW]
                    if dr != 1:
                        wv = wv * lat
                    term = tap * wv
                    if dc == 0:
                        term = jnp.where(left_ok, term, 0.0)
                    elif dc == 2:
                        term = jnp.where(right_ok, term, 0.0)
                    acc = acc + term
            gate = acc + dwb_ref[:, n0:n0 + FFN_CW]
            a = _silu(gate) * val_ref[0, rs * gw:(rs + 1) * gw, n0:n0 + FFN_CW]
            act_scr[rs * gw:(rs + 1) * gw, n0:n0 + FFN_CW] = a.astype(BF16)

    out = jnp.dot(act_scr[...], wd_ref[...], preferred_element_type=F32)
    g2 = mod_ref[0, 0][:, mod_idx * d:(mod_idx + 1) * d]
    x2 = x_ref[0] + g2 * out
    if final:
        ms = jnp.mean(x2 * x2, axis=-1, keepdims=True)
        x2 = x2 * lax.rsqrt(ms + EPS) * fg_ref[...]
    o_ref[0] = x2


def _ffn_down(gate, val, xs, mod, dw9, dw_b, w_down, *, tile0=0, final_g=None):
    bsz, s, f = gate.shape
    d = xs.shape[-1]
    tm = TOKEN_TILE
    gw = GRID_W
    n_all = s // tm
    per = tm // gw
    n_rows = s // gw
    final = final_g is not None
    tok = lambda width: pl.BlockSpec((1, tm, width), lambda i, j: (i, j + tile0, 0))
    in_specs = [tok(f),
                pl.BlockSpec((1, gw, f), lambda i, j: (i, jnp.maximum((j + tile0) * per - 1, tile0 * per), 0)),
                pl.BlockSpec((1, gw, f), lambda i, j: (i, jnp.minimum((j + tile0 + 1) * per, n_rows - 1), 0)),
                tok(f), tok(d),
                pl.BlockSpec((1, 1, 1, mod.shape[-1]), lambda i, j: (i, jnp.minimum(j + tile0, 1), 0, 0)),
                pl.BlockSpec((9, f), lambda i, j: (0, 0)),
                pl.BlockSpec((1, f), lambda i, j: (0, 0)),
                pl.BlockSpec(w_down.shape, lambda i, j: (0, 0))]
    args = [gate, gate, gate, val, xs, mod, dw9, dw_b.reshape(1, f), w_down]
    if final:
        in_specs.append(pl.BlockSpec((1, d), lambda i, j: (0, 0)))
        args.append(final_g.reshape(1, d))
        out_shape = jax.ShapeDtypeStruct((bsz, s - tile0 * tm, d), F32)
        out_spec = pl.BlockSpec((1, tm, d), lambda i, j: (i, j, 0))
    else:
        out_shape = jax.ShapeDtypeStruct((bsz, s, d), F32)
        out_spec = tok(d)
    kern = functools.partial(_ffn_down_kernel, tile0=tile0, n_tiles_all=n_all, final=final, mod_idx=5)
    return pl.pallas_call(
        kern, out_shape=out_shape, grid=(bsz, n_all - tile0), in_specs=in_specs, out_specs=out_spec,
        scratch_shapes=[pltpu.VMEM((tm + 2 * gw + 16, f), F32), pltpu.VMEM((tm, f), BF16)],
        compiler_params=_cparams(("arbitrary", "arbitrary")), name="convglu_down",
    )(*args)


def _head_major_rows(w_cols, n_heads):
    d = w_cols.shape[0]
    w = w_cols.reshape(d, 4, n_heads).transpose(2, 1, 0)
    w = jnp.concatenate([w, jnp.zeros((n_heads, GATE_ROWS - 4, d), w.dtype)], axis=1)
    return w.reshape(n_heads * GATE_ROWS, d)


def _head_major_vec(b_cols, n_heads):
    b = b_cols.reshape(4, n_heads).T
    b = jnp.concatenate([b, jnp.zeros((n_heads, GATE_ROWS - 4), b.dtype)], axis=1)
    return b.reshape(n_heads * GATE_ROWS)


def kernel(x, c, ctx, c_ctx, ada_w, ada_b, norm1_g, norm2_g, e_w_in, e_b_in, e_head_g, e_conf_dw, e_conf_dw_b, e_conf_ln_g, e_conf_ln_b, e_w_out, o_w_in, o_short_w, o_a_log, o_dt_bias, o_head_g, o_w_out, f_w_up, f_dw, f_dw_b, f_w_down, final_g):
    bsz, t, d = x.shape
    sc = ctx.shape[1]
    depth = ada_w.shape[0]
    assert sc == TOKEN_TILE and t % TOKEN_TILE == 0 and t % CHUNK == 0 and sc % CHUNK == 0
    s = sc + t
    n_ctx, n_all = sc // CHUNK, s // CHUNK
    mw = MLSTM_HEADS * LANES
    gwd = GDN_HEADS * LANES
    ffn = f_w_down.shape[1]

    xs = jnp.concatenate([ctx, x], axis=1)

    rows = 16
    c_rows = jnp.concatenate([c, c_ctx[None, :], jnp.zeros((rows - bsz - 1, d), F32)], axis=0)
    mod_all = _ada_mod(c_rows, ada_w, ada_b)

    for layer in range(depth):
        last = layer == depth - 1
        tile0 = 1 if last else 0
        ml = mod_all[layer]
        mod = jnp.stack([jnp.broadcast_to(ml[bsz][None], (bsz, 6 * d)), ml[:bsz]], axis=1)[:, :, None, :]
        jj = layer // 2
        if layer % 2 == 0:
            w_in, b_in = e_w_in[jj], e_b_in[jj]
            g0 = 4 * mw
            g1 = g0 + 4 * MLSTM_HEADS
            w_main = jnp.concatenate([w_in[:, :g0], w_in[:, g1:]], axis=1).astype(BF16)
            b_main = jnp.concatenate([b_in[:g0], b_in[g1:]])
            w_rows = _head_major_rows(w_in[:, g0:g1], MLSTM_HEADS).astype(BF16)
            b_rows = _head_major_vec(b_in[g0:g1], MLSTM_HEADS)
            qkvo, glu, gates_t = _proj(xs, mod, norm1_g[layer], w_main, b_main, mod_idx=0,
                                       seg_widths=(g0, w_in.shape[1] - g1), w_rows=w_rows, b_rows=b_rows)
            y = _mlstm(qkvo, gates_t, n_ctx=n_ctx, n_all=n_all)
            c_out = _conformer(glu, e_conf_dw[jj], e_conf_dw_b[jj], e_conf_ln_g[jj], e_conf_ln_b[jj], sc=sc)
            x1 = _finish(y, qkvo, 3, c_out, xs, mod, e_head_g[jj], e_w_out[jj].astype(BF16),
                         n_heads=MLSTM_HEADS, tile0=tile0)
        else:
            w_in = o_w_in[jj]
            g0 = 4 * gwd
            w_main = w_in[:, :g0].astype(BF16)
            w_rows = _head_major_rows(w_in[:, g0:], GDN_HEADS).astype(BF16)
            b_rows = jnp.zeros((GDN_HEADS * GATE_ROWS,), F32)
            qkv, z, ab_t = _proj(xs, mod, norm1_g[layer], w_main, jnp.zeros((g0,), F32), mod_idx=0,
                                 seg_widths=(3 * gwd, gwd), w_rows=w_rows, b_rows=b_rows)
            zpad = jnp.zeros((GDN_HEADS, GATE_ROWS - 3), F32)
            alog_rows = jnp.concatenate([o_a_log[jj][0][:, None], jnp.zeros((GDN_HEADS, 1), F32),
                                         o_a_log[jj][1][:, None], zpad], axis=1).reshape(-1, 1)
            dtb_rows = jnp.concatenate([o_dt_bias[jj][0][:, None], jnp.zeros((GDN_HEADS, 1), F32),
                                        o_dt_bias[jj][1][:, None], zpad], axis=1).reshape(-1, 1)
            y = _gdn(qkv, o_short_w[jj], ab_t, alog_rows, dtb_rows, n_ctx=n_ctx, n_all=n_all)
            x1 = _finish(y, z, 0, None, xs, mod, jnp.tile(o_head_g[jj], GDN_HEADS), o_w_out[jj].astype(BF16),
                         n_heads=GDN_HEADS, tile0=tile0)
        gate, val = _proj(x1, mod, norm2_g[layer], f_w_up[layer].astype(BF16), jnp.zeros((2 * ffn,), F32),
                          mod_idx=3, seg_widths=(ffn, ffn), tile0=tile0)
        xs = _ffn_down(gate, val, x1, mod, f_dw[layer].reshape(9, ffn), f_dw_b[layer],
                       f_w_down[layer].astype(BF16), tile0=tile0, final_g=final_g if last else None)
    return xs
```

```python
import functools

import jax
import jax.numpy as jnp
from jax import lax
from jax.experimental import pallas as pl
from jax.experimental.pallas import tpu as pltpu

F32 = jnp.float32
BF16 = jnp.bfloat16
EPS = 1e-6

LANES = 128
CHUNK = LANES
TOKEN_TILE = 256
GRID_W = 64
MLSTM_HEADS = 4
GDN_HEADS = 8
CONF_K = 31
SHORT_K = 5
GATE_ROWS = 8
SCAN_HEADS_PER_STEP = 2
FFN_CW = 256
VMEM_LIMIT = 56 * 1024 * 1024


def _cparams(sem):
    return pltpu.CompilerParams(dimension_semantics=sem, vmem_limit_bytes=VMEM_LIMIT)


def _sigmoid(x):
    return 1.0 / (1.0 + jnp.exp(-x))


def _silu(x):
    return x * _sigmoid(x)


def _softplus(x):
    return jnp.maximum(x, 0.0) + jnp.log1p(jnp.exp(-jnp.abs(x)))


def _log_sigmoid(x):
    return jnp.minimum(x, 0.0) - jnp.log1p(jnp.exp(-jnp.abs(x)))


def _mm(a, b):
    return jnp.dot(a.astype(BF16), b.astype(BF16), preferred_element_type=F32)


def _mm_nt(a, b):
    return lax.dot_general(a.astype(BF16), b.astype(BF16), (((1,), (1,)), ((), ())),
                           preferred_element_type=F32)


def _mm_tn(a, b):
    return lax.dot_general(a.astype(BF16), b.astype(BF16), (((0,), (0,)), ((), ())),
                           preferred_element_type=F32)


def _split2(x):
    hi = x.astype(BF16)
    lo = (x - hi.astype(F32)).astype(BF16)
    return hi, lo


def _split3(x):
    hi = x.astype(BF16)
    r = x - hi.astype(F32)
    mid = r.astype(BF16)
    lo = (r - mid.astype(F32)).astype(BF16)
    return hi, mid, lo


def _mm_x01(x, m01):
    hi, mid, lo = _split3(x)
    dot = functools.partial(jnp.dot, preferred_element_type=F32)
    return dot(hi, m01) + dot(mid, m01) + dot(lo, m01)


def _mm3(a, b):
    ah, al = _split2(a)
    bh, bl = _split2(b)
    dot = functools.partial(jnp.dot, preferred_element_type=F32)
    return dot(ah, bh) + dot(ah, bl) + dot(al, bh)


def _chunk_masks(L):
    row = lax.broadcasted_iota(jnp.int32, (L, L), 0)
    col = lax.broadcasted_iota(jnp.int32, (L, L), 1)
    return row, col


def _ada_kernel(c_ref, w_ref, b_ref, o_ref):
    c = c_ref[...]
    o_ref[0] = _mm(_silu(c), w_ref[0]) + b_ref[0]


def _ada_mod(c_rows, ada_w, ada_b):
    depth, d, n = ada_w.shape
    tn = 1536
    assert n % tn == 0
    rows = c_rows.shape[0]
    return pl.pallas_call(
        _ada_kernel,
        out_shape=jax.ShapeDtypeStruct((depth, rows, n), F32),
        grid=(depth, n // tn),
        in_specs=[pl.BlockSpec((rows, d), lambda l, j: (0, 0)),
                  pl.BlockSpec((1, d, tn), lambda l, j: (l, 0, j)),
                  pl.BlockSpec((1, 1, tn), lambda l, j: (l, 0, j))],
        out_specs=pl.BlockSpec((1, rows, tn), lambda l, j: (l, 0, j)),
        compiler_params=_cparams(("arbitrary", "arbitrary")),
        name="ada_mod",
    )(c_rows, ada_w, ada_b.reshape(depth, 1, n))


def _proj_kernel(x_ref, mod_ref, g_ref, w_ref, b_ref, *rest, mod_idx, segs, with_rows, tn):
    d = x_ref.shape[-1]
    if with_rows:
        wr_ref, br_ref = rest[:2]
        outs = rest[2:]
    else:
        outs = rest
    x = x_ref[0]
    ms = jnp.mean(x * x, axis=-1, keepdims=True)
    r = x * lax.rsqrt(ms + EPS)
    mod = mod_ref[0, 0]
    shift = mod[:, mod_idx * d:(mod_idx + 1) * d]
    scale = mod[:, (mod_idx + 1) * d:(mod_idx + 2) * d]
    h = r * g_ref[...] * (1.0 + scale) + shift
    hb = h.astype(BF16)
    for (off, width), o_ref in zip(segs, outs):
        for n0 in range(0, width, tn):
            nw = min(tn, width - n0)
            acc = jnp.dot(hb, w_ref[:, off + n0:off + n0 + nw], preferred_element_type=F32)
            o_ref[0, :, n0:n0 + nw] = acc + b_ref[:, off + n0:off + n0 + nw]
    if with_rows:
        rows = lax.dot_general(wr_ref[...], hb, (((1,), (1,)), ((), ())), preferred_element_type=F32)
        outs[-1][0] = rows + br_ref[...]


def _proj(xs, mod, g, w, b, *, mod_idx, seg_widths, w_rows=None, b_rows=None, tile0=0):
    bsz, s, d = xs.shape
    tm = TOKEN_TILE
    n_tiles = s // tm - tile0
    segs, off = [], 0
    for wd in seg_widths:
        segs.append((off, wd))
        off += wd
    assert off == w.shape[1]
    with_rows = w_rows is not None
    in_specs = [pl.BlockSpec((1, tm, d), lambda i, j: (i, j + tile0, 0)),
                pl.BlockSpec((1, 1, 1, mod.shape[-1]), lambda i, j: (i, jnp.minimum(j + tile0, 1), 0, 0)),
                pl.BlockSpec((1, d), lambda i, j: (0, 0)),
                pl.BlockSpec(w.shape, lambda i, j: (0, 0)),
                pl.BlockSpec((1, w.shape[1]), lambda i, j: (0, 0))]
    args = [xs, mod, g.reshape(1, d), w, b.reshape(1, -1)]
    out_shape = [jax.ShapeDtypeStruct((bsz, s, wd), F32) for wd in seg_widths]
    out_specs = [pl.BlockSpec((1, tm, wd), lambda i, j: (i, j + tile0, 0)) for wd in seg_widths]
    if with_rows:
        nr = w_rows.shape[0]
        in_specs += [pl.BlockSpec(w_rows.shape, lambda i, j: (0, 0)),
                     pl.BlockSpec((nr, 1), lambda i, j: (0, 0))]
        args += [w_rows, b_rows.reshape(nr, 1)]
        out_shape.append(jax.ShapeDtypeStruct((bsz, nr, s), F32))
        out_specs.append(pl.BlockSpec((1, nr, tm), lambda i, j: (i, 0, j + tile0)))
    kern = functools.partial(_proj_kernel, mod_idx=mod_idx, segs=tuple(segs), with_rows=with_rows, tn=512)
    return pl.pallas_call(
        kern, out_shape=out_shape, grid=(bsz, n_tiles), in_specs=in_specs, out_specs=out_specs,
        compiler_params=_cparams(("arbitrary", "arbitrary")), name="norm_proj",
    )(*args)


def _bwd_chunk(i, n_ctx, n_all):
    return jnp.where(i < n_ctx, n_ctx - 1 - i, n_all - 1 - (i - n_ctx))


def _col_bcast(masked_rows, ones_b):
    return _mm_x01(masked_rows, ones_b)


def _mlstm_kernel(q_ref, k_ref, v_ref, g_ref, y_ref, c_scr, *, n_ctx, n_all, hp):
    L = CHUNK
    row, col = _chunk_masks(L)
    incl = (col <= row, col >= row)
    csum_r = (incl[1].astype(BF16), incl[0].astype(BF16))
    eye = row == col
    ones_b = jnp.ones((L, LANES), BF16)
    scale = float(LANES) ** -0.5

    y_ref[...] = jnp.zeros(y_ref.shape, F32)
    c_scr[...] = jnp.zeros(c_scr.shape, F32)

    chains = [(h, d) for h in range(hp) for d in (0, 1)]
    lanes = lambda h: slice(h * LANES, (h + 1) * LANES)

    def body(i, carry):
        n0, m0 = carry
        r0 = (pl.multiple_of(i * L, L), pl.multiple_of(_bwd_chunk(i, n_ctx, n_all) * L, L))
        gt = [g_ref[0, :, pl.ds(r0[d], L)] for d in (0, 1)]
        lsr = [_log_sigmoid(g) for g in gt]
        bsum = [_mm_x01(lsr[d], csum_r[d]) for d in (0, 1)]
        row = lambda arr, h, d, kind: arr[d][h * GATE_ROWS + 2 * d + kind:h * GATE_ROWS + 2 * d + kind + 1, :]
        li_r = [row(gt, h, d, 0) for h, d in chains]
        lf_r = [row(lsr, h, d, 1) for h, d in chains]
        b_r = [row(bsum, h, d, 1) for h, d in chains]
        q = [q_ref[0, pl.ds(r0[d], L), lanes(h)] * scale for h, d in chains]
        k = [k_ref[0, pl.ds(r0[d], L), lanes(h)] for h, d in chains]
        v = [v_ref[0, pl.ds(r0[d], L), lanes(h)] for h, d in chains]
        b_c = [_col_bcast(jnp.where(incl[d], f, 0.0), ones_b) for (h, d), f in zip(chains, lf_r)]
        li_c = [_col_bcast(jnp.where(eye, x, 0.0), ones_b) for x in li_r]
        total = [b[:, L - 1:L] if d == 0 else b[:, 0:1] for (h, d), b in zip(chains, b_r)]
        c0 = [c_scr[2 * h + d] for h, d in chains]
        m_loc = [jnp.max(t - b + x, axis=1, keepdims=True) for t, b, x in zip(total, b_r, li_r)]
        kw = [kx * jnp.exp(t - b + x - m) for kx, t, b, x, m in zip(k, total, b_c, li_c, m_loc)]
        c_loc = [_mm_tn(a, vv) for a, vv in zip(kw, v)]
        m_new = [jnp.maximum(t + m, ml) for t, m, ml in zip(total, m0, m_loc)]
        a_old = [jnp.exp(t + m - mn) for t, m, mn in zip(total, m0, m_new)]
        a_loc = [jnp.exp(ml - mn) for ml, mn in zip(m_loc, m_new)]
        for (h, d), ao, c, al, cl in zip(chains, a_old, c0, a_loc, c_loc):
            c_scr[2 * h + d] = ao * c + al * cl
        n_new = [ao * n + al * jnp.sum(a, axis=0, keepdims=True)
                 for ao, n, al, a in zip(a_old, n0, a_loc, kw)]
        dlog = [jnp.where(incl[d], bc - br + x, -jnp.inf) for (h, d), bc, br, x in zip(chains, b_c, b_r, li_r)]
        inter = [bc + m for bc, m in zip(b_c, m0)]
        m_t = [jnp.maximum(x, jnp.max(dl, axis=1, keepdims=True)) for x, dl in zip(inter, dlog)]
        s = [_mm_nt(qq, kx) for qq, kx in zip(q, k)]
        p = [jnp.exp(dl - mt) * ss for dl, mt, ss in zip(dlog, m_t, s)]
        a_int = [jnp.exp(x - mt) for x, mt in zip(inter, m_t)]
        num = [ai * _mm(qq, c) + _mm(pp, vv) for ai, qq, c, pp, vv in zip(a_int, q, c0, p, v)]
        den = [ai * jnp.sum(qq * n, axis=1, keepdims=True) + jnp.sum(pp, axis=1, keepdims=True)
               for ai, qq, n, pp in zip(a_int, q, n0, p)]
        for (h, d), nu, de, mt in zip(chains, num, den, m_t):
            y_ref[0, pl.ds(r0[d], L), lanes(h)] += nu / jnp.maximum(jnp.abs(de), jnp.exp(-mt))
        return tuple(n_new), tuple(m_new)

    zn = tuple(jnp.zeros((1, LANES), F32) for _ in chains)
    zm = tuple(jnp.zeros((1, 1), F32) for _ in chains)
    lax.fori_loop(0, n_all, body, (zn, zm))


def _mlstm(qkvo, gates_t, *, n_ctx, n_all):
    bsz, s, _ = qkvo.shape
    h = MLSTM_HEADS
    hp = SCAN_HEADS_PER_STEP
    ng = h // hp
    wd = hp * LANES
    kern = functools.partial(_mlstm_kernel, n_ctx=n_ctx, n_all=n_all, hp=hp)
    blk = lambda off: pl.BlockSpec((1, s, wd), lambda i, j: (i, 0, j + off))
    return pl.pallas_call(
        kern, out_shape=jax.ShapeDtypeStruct((bsz, s, h * LANES), F32), grid=(bsz, ng),
        in_specs=[blk(0), blk(ng), blk(2 * ng),
                  pl.BlockSpec((1, hp * GATE_ROWS, s), lambda i, j: (i, j, 0))],
        out_specs=pl.BlockSpec((1, s, wd), lambda i, j: (i, 0, j)),
        scratch_shapes=[pltpu.VMEM((2 * hp, LANES, LANES), F32)],
        compiler_params=_cparams(("arbitrary", "arbitrary")), name="mlstm_scan",
    )(qkvo, qkvo, qkvo, gates_t)


def _gdn_kernel(q_ref, k_ref, v_ref, wq_ref, wk_ref, wv_ref, ab_ref, alog_ref, dtb_ref, y_ref,
                xp_scr, q_scr, k_scr, v_scr, s_scr, *, n_ctx, n_all, hp):
    L = CHUNK
    sc = n_ctx * L
    s_len = n_all * L
    row, col = _chunk_masks(L)
    incl = (col <= row, col >= row)
    strict = (col < row, col > row)
    csum_r = (incl[1].astype(BF16), incl[0].astype(BF16))
    eye = row == col
    eye_f = jnp.where(eye, 1.0, 0.0)
    blk = tuple(jnp.right_shift(row, sh) == jnp.right_shift(col, sh) for sh in (4, 5, 6, 7))
    ones_b = jnp.ones((L, LANES), BF16)
    qscale = float(LANES) ** -0.5
    pad = 8

    width = hp * LANES
    zeros_pad = jnp.zeros((pad, width), F32)
    xp_scr[0:pad, :] = zeros_pad
    xp_scr[pad + sc:2 * pad + sc, :] = zeros_pad
    xp_scr[2 * pad + s_len:3 * pad + s_len, :] = zeros_pad
    half = SHORT_K // 2
    for src, w_ref, dst, norm in ((q_ref, wq_ref, q_scr, True), (k_ref, wk_ref, k_scr, True),
                                  (v_ref, wv_ref, v_scr, False)):
        xp_scr[pad:pad + sc, :] = src[0, 0:sc, :]
        xp_scr[2 * pad + sc:2 * pad + s_len, :] = src[0, sc:s_len, :]
        for c in range(n_all):
            base = c * L + (pad if c < n_ctx else 2 * pad)
            acc = jnp.zeros((L, width), F32)
            for j in range(SHORT_K):
                acc = acc + w_ref[j:j + 1, :] * xp_scr[base + j - half:base + j - half + L, :]
            a = _silu(acc)
            for h in range(hp):
                ah = a[:, h * LANES:(h + 1) * LANES]
                if norm:
                    ah = ah * lax.rsqrt(jnp.sum(ah * ah, axis=-1, keepdims=True) + EPS)
                dst[c * L:(c + 1) * L, h * LANES:(h + 1) * LANES] = ah

    y_ref[...] = jnp.zeros(y_ref.shape, F32)
    s_scr[...] = jnp.zeros(s_scr.shape, F32)
    arate = jnp.exp(alog_ref[...])
    dtb = dtb_ref[...]

    chains = [(h, d) for h in range(hp) for d in (0, 1)]
    off_masks = [jnp.logical_and(blk[lvl], jnp.logical_not(blk[lvl - 1])) for lvl in range(1, len(blk))]

    def body(i, carry):
        r0 = (pl.multiple_of(i * L, L), pl.multiple_of(_bwd_chunk(i, n_ctx, n_all) * L, L))
        ab = [ab_ref[0, :, pl.ds(r0[d], L)] for d in (0, 1)]
        g8 = [-arate * _softplus(a + dtb) for a in ab]
        beta8 = [_sigmoid(a) for a in ab]
        gcs = [_mm_x01(g8[d], csum_r[d]) for d in (0, 1)]
        row = lambda arr, h, d, kind: arr[d][h * GATE_ROWS + 2 * d + kind:h * GATE_ROWS + 2 * d + kind + 1, :]
        g_r = [row(g8, h, d, 0) for h, d in chains]
        gc_r = [row(gcs, h, d, 0) for h, d in chains]
        beta_r = [row(beta8, h, d, 1) for h, d in chains]
        lanes = lambda h: slice(h * LANES, (h + 1) * LANES)
        q = [q_scr[pl.ds(r0[d], L), lanes(h)] * qscale for h, d in chains]
        k = [k_scr[pl.ds(r0[d], L), lanes(h)] for h, d in chains]
        v = [v_scr[pl.ds(r0[d], L), lanes(h)] for h, d in chains]
        gc_c = [_col_bcast(jnp.where(incl[d], g, 0.0), ones_b) for (h, d), g in zip(chains, g_r)]
        beta_c = [_col_bcast(jnp.where(eye, b, 0.0), ones_b) for b in beta_r]
        total = [g[:, L - 1:L] if d == 0 else g[:, 0:1] for (h, d), g in zip(chains, gc_r)]
        dec = [jnp.exp(jnp.where(incl[d], c - r, -jnp.inf)) for (h, d), c, r in zip(chains, gc_c, gc_r)]
        kk = [_mm_nt(x, x) for x in k]
        a_mat = [jnp.where(strict[d], b * m * e, 0.0) for (h, d), b, m, e in zip(chains, beta_c, kk, dec)]
        egc = [jnp.exp(c) for c in gc_c]
        rhs = [jnp.concatenate([b * vv, (b * e) * kx], axis=1)
               for b, vv, e, kx in zip(beta_c, v, egc, k)]
        pw = [-jnp.where(blk[0], a, 0.0) for a in a_mat]
        t_inv = [eye_f + p for p in pw]
        for _ in range(3):
            pw = [_mm3(p, p) for p in pw]
            t_inv = [t + _mm3(t, p) for t, p in zip(t_inv, pw)]
        for om in off_masks:
            ta = [_mm3(t, jnp.where(om, a, 0.0)) for t, a in zip(t_inv, a_mat)]
            t_inv = [t - _mm3(x, t) for t, x in zip(t_inv, ta)]
        sol = [_mm3(t, x) for t, x in zip(t_inv, rhs)]
        s0 = [s_scr[2 * h + d] for h, d in chains]
        v_new = [x[:, :LANES] - _mm(x[:, LANES:], s) for x, s in zip(sol, s0)]
        kdec = [jnp.exp(t - c) * kx for t, c, kx in zip(total, gc_c, k)]
        upd = [_mm_tn(kd, vn) for kd, vn in zip(kdec, v_new)]
        for (h, d), t, s, u in zip(chains, total, s0, upd):
            s_scr[2 * h + d] = jnp.exp(t) * s + u
        qk = [_mm_nt(qq, kx) * e for qq, kx, e in zip(q, k, dec)]
        o = [_mm(e * qq, s) + _mm(m, vn) for e, qq, s, m, vn in zip(egc, q, s0, qk, v_new)]
        for (h, d), oo in zip(chains, o):
            y_ref[0, pl.ds(r0[d], L), lanes(h)] += oo
        return carry

    lax.fori_loop(0, n_all, body, 0)


def _gdn(qkv, short_w, ab_t, alog_rows, dtb_rows, *, n_ctx, n_all):
    bsz, s, _ = qkv.shape
    h = GDN_HEADS
    hp = SCAN_HEADS_PER_STEP
    ng = h // hp
    wd = hp * LANES
    kern = functools.partial(_gdn_kernel, n_ctx=n_ctx, n_all=n_all, hp=hp)
    blk = lambda off: pl.BlockSpec((1, s, wd), lambda i, j: (i, 0, j + off))
    wblk = lambda off: pl.BlockSpec((SHORT_K, wd), lambda i, j: (0, j + off))
    return pl.pallas_call(
        kern, out_shape=jax.ShapeDtypeStruct((bsz, s, h * LANES), F32), grid=(bsz, ng),
        in_specs=[blk(0), blk(ng), blk(2 * ng), wblk(0), wblk(ng), wblk(2 * ng),
                  pl.BlockSpec((1, hp * GATE_ROWS, s), lambda i, j: (i, j, 0)),
                  pl.BlockSpec((hp * GATE_ROWS, 1), lambda i, j: (j, 0)),
                  pl.BlockSpec((hp * GATE_ROWS, 1), lambda i, j: (j, 0))],
        out_specs=pl.BlockSpec((1, s, wd), lambda i, j: (i, 0, j)),
        scratch_shapes=[pltpu.VMEM((s + 24, wd), F32), pltpu.VMEM((s, wd), F32),
                        pltpu.VMEM((s, wd), F32), pltpu.VMEM((s, wd), F32),
                        pltpu.VMEM((2 * hp, LANES, LANES), F32)],
        compiler_params=_cparams(("arbitrary", "arbitrary")), name="gdn_scan",
    )(qkv, qkv, qkv, short_w, short_w, short_w, ab_t, alog_rows, dtb_rows)


def _conformer_kernel(glu_ref, w_ref, b_ref, lg_ref, lb_ref, o_ref, u_scr, *, sc, s_len):
    ch = o_ref.shape[-1]
    pad = 16
    rt = 32
    half = CONF_K // 2
    zeros_pad = jnp.zeros((pad, ch), F32)
    u_scr[0:pad, :] = zeros_pad
    u_scr[pad + sc:2 * pad + sc, :] = zeros_pad
    u_scr[2 * pad + s_len:3 * pad + s_len, :] = zeros_pad

    def fill(i, carry):
        r0 = pl.multiple_of(i * rt, rt)
        off = jnp.where(r0 < sc, pad, 2 * pad)
        a = glu_ref[0, pl.ds(r0, rt), 0:ch]
        gt = glu_ref[0, pl.ds(r0, rt), ch:2 * ch]
        u_scr[pl.ds(pl.multiple_of(r0 + off, 8), rt), :] = a * _sigmoid(gt)
        return carry

    lax.fori_loop(0, s_len // rt, fill, 0)

    def conv(i, carry):
        r0 = pl.multiple_of(i * rt, rt)
        off = jnp.where(r0 < sc, pad, 2 * pad)
        win = u_scr[pl.ds(pl.multiple_of(r0 + off - pad, 8), rt + 2 * pad), :]
        nwin = rt + 2 * pad
        acc = jnp.zeros((rt, ch), F32)
        for res in range(8):
            shifted = win if res == 0 else pltpu.roll(win, nwin - res, axis=0)
            for a8 in range(0, nwin - rt + 1, 8):
                j = a8 + res - (pad - half)
                if 0 <= j < CONF_K:
                    acc = acc + w_ref[j:j + 1, :] * shifted[a8:a8 + rt, :]
        y = acc + b_ref[...]
        mu = jnp.mean(y, axis=-1, keepdims=True)
        yc = y - mu
        var = jnp.mean(yc * yc, axis=-1, keepdims=True)
        z = yc * lax.rsqrt(var + EPS) * lg_ref[...] + lb_ref[...]
        o_ref[0, pl.ds(r0, rt), :] = _silu(z)
        return carry

    lax.fori_loop(0, s_len // rt, conv, 0)


def _conformer(glu, dw, dw_b, ln_g, ln_b, *, sc):
    bsz, s, c2 = glu.shape
    ch = c2 // 2
    kern = functools.partial(_conformer_kernel, sc=sc, s_len=s)
    vec = lambda: pl.BlockSpec((1, ch), lambda i: (0, 0))
    return pl.pallas_call(
        kern, out_shape=jax.ShapeDtypeStruct((bsz, s, ch), F32), grid=(bsz,),
        in_specs=[pl.BlockSpec((1, s, c2), lambda i: (i, 0, 0)),
                  pl.BlockSpec((CONF_K, ch), lambda i: (0, 0)), vec(), vec(), vec()],
        out_specs=pl.BlockSpec((1, s, ch), lambda i: (i, 0, 0)),
        scratch_shapes=[pltpu.VMEM((s + 48, ch), F32)],
        compiler_params=_cparams(("arbitrary",)), name="conformer_conv",
    )(glu, dw, dw_b.reshape(1, ch), ln_g.reshape(1, ch), ln_b.reshape(1, ch))


def _finish_kernel(y_ref, gate_ref, *rest, n_heads, even, mod_idx):
    if even:
        c_ref, x_ref, mod_ref, hg_ref, w_ref, o_ref = rest
    else:
        x_ref, mod_ref, hg_ref, w_ref, o_ref = rest
    d = x_ref.shape[-1]
    y = y_ref[0]
    parts = []
    for h in range(n_heads):
        yh = y[:, h * LANES:(h + 1) * LANES]
        ms = jnp.mean(yh * yh, axis=-1, keepdims=True)
        parts.append(yh * lax.rsqrt(ms + EPS))
    yn = jnp.concatenate(parts, axis=1)
    gate = gate_ref[0]
    if even:
        m = yn * hg_ref[...] * _sigmoid(gate)
        cat = jnp.concatenate([m.astype(BF16), c_ref[0].astype(BF16)], axis=1)
    else:
        cat = ((yn * hg_ref[...]) * _silu(gate)).astype(BF16)
    out = jnp.dot(cat, w_ref[...], preferred_element_type=F32)
    g1 = mod_ref[0, 0][:, mod_idx * d:(mod_idx + 1) * d]
    o_ref[0] = x_ref[0] + g1 * out


def _finish(y, gate_src, gate_blk, c_out, xs, mod, head_g, w_out, *, n_heads, tile0=0):
    bsz, s, d = xs.shape
    tm = TOKEN_TILE
    wdt = n_heads * LANES
    even = c_out is not None
    tok = lambda width, blk=0: pl.BlockSpec((1, tm, width), lambda i, j: (i, j + tile0, blk))
    in_specs = [tok(wdt), tok(wdt, gate_blk)]
    args = [y, gate_src]
    if even:
        in_specs.append(tok(c_out.shape[-1]))
        args.append(c_out)
    in_specs += [tok(d),
                 pl.BlockSpec((1, 1, 1, mod.shape[-1]), lambda i, j: (i, jnp.minimum(j + tile0, 1), 0, 0)),
                 pl.BlockSpec((1, wdt), lambda i, j: (0, 0)),
                 pl.BlockSpec(w_out.shape, lambda i, j: (0, 0))]
    args += [xs, mod, head_g.reshape(1, wdt), w_out]
    kern = functools.partial(_finish_kernel, n_heads=n_heads, even=even, mod_idx=2)
    return pl.pallas_call(
        kern, out_shape=jax.ShapeDtypeStruct((bsz, s, d), F32), grid=(bsz, s // tm - tile0),
        in_specs=in_specs, out_specs=tok(d),
        compiler_params=_cparams(("arbitrary", "arbitrary")), name="mixer_out",
    )(*args)


def _ffn_down_kernel(gm_ref, gp_ref, gn_ref, val_ref, x_ref, mod_ref, dw_ref, dwb_ref, wd_ref, *rest,
                     tile0, n_tiles_all, final, mod_idx):
    if final:
        fg_ref, o_ref, ext_scr, act_scr = rest
    else:
        o_ref, ext_scr, act_scr = rest
    tm = TOKEN_TILE
    gw = GRID_W
    f = gm_ref.shape[-1]
    d = x_ref.shape[-1]
    j = pl.program_id(1) + tile0
    lat = jnp.where(j > 0, 1.0, 0.0)
    jrow = jnp.broadcast_to(j, (gw, 1))
    has_prev = jrow >= 2
    has_next = jnp.logical_and(jrow >= 1, jrow <= n_tiles_all - 2)
    top = 8
    main = top + gw
    ext_scr[0:top, :] = jnp.zeros((top, f), F32)
    ext_scr[main + tm + gw:main + tm + gw + top, :] = jnp.zeros((top, f), F32)
    ext_scr[top:main, :] = jnp.where(has_prev, gp_ref[0], 0.0)
    ext_scr[main:main + tm, :] = gm_ref[0]
    ext_scr[main + tm:main + tm + gw, :] = jnp.where(has_next, gn_ref[0], 0.0)

    cidx = lax.broadcasted_iota(jnp.int32, (gw, 1), 0)
    jv = jrow
    for rs in range(tm // gw):
        left_ok = jnp.logical_or(cidx != 0, jv == 0) if rs > 0 else cidx != 0
        right_ok = jnp.logical_or(cidx != gw - 1, jv == 0) if rs < tm // gw - 1 else cidx != gw - 1
        for n0 in range(0, f, FFN_CW):
            base = main + rs * gw
            acc = jnp.zeros((gw, FFN_CW), F32)
            for dr in range(3):
                for dc in range(3):
                    st = base + (dr - 1) * gw + (dc - 1)
                    tap = ext_scr[st:st + gw, n0:n0 + FFN_CW]
                    wv = dw_ref[dr * 3 + dc:dr * 3 + dc + 1, n0:n0 + FFN_CW]
                    if dr != 1:
                        wv = wv * lat
                    term = tap * wv
                    if dc == 0:
                        term = jnp.where(left_ok, term, 0.0)
                    elif dc == 2:
                        term = jnp.where(right_ok, term, 0.0)
                    acc = acc + term
            gate = acc + dwb_ref[:, n0:n0 + FFN_CW]
            a = _silu(gate) * val_ref[0, rs * gw:(rs + 1) * gw, n0:n0 + FFN_CW]
            act_scr[rs * gw:(rs + 1) * gw, n0:n0 + FFN_CW] = a.astype(BF16)

    out = jnp.dot(act_scr[...], wd_ref[...], preferred_element_type=F32)
    g2 = mod_ref[0, 0][:, mod_idx * d:(mod_idx + 1) * d]
    x2 = x_ref[0] + g2 * out
    if final:
        ms = jnp.mean(x2 * x2, axis=-1, keepdims=True)
        x2 = x2 * lax.rsqrt(ms + EPS) * fg_ref[...]
    o_ref[0] = x2


def _ffn_down(gate, val, xs, mod, dw9, dw_b, w_down, *, tile0=0, final_g=None):
    bsz, s, f = gate.shape
    d = xs.shape[-1]
    tm = TOKEN_TILE
    gw = GRID_W
    n_all = s // tm
    per = tm // gw
    n_rows = s // gw
    final = final_g is not None
    tok = lambda width: pl.BlockSpec((1, tm, width), lambda i, j: (i, j + tile0, 0))
    in_specs = [tok(f),
                pl.BlockSpec((1, gw, f), lambda i, j: (i, jnp.maximum((j + tile0) * per - 1, tile0 * per), 0)),
                pl.BlockSpec((1, gw, f), lambda i, j: (i, jnp.minimum((j + tile0 + 1) * per, n_rows - 1), 0)),
                tok(f), tok(d),
                pl.BlockSpec((1, 1, 1, mod.shape[-1]), lambda i, j: (i, jnp.minimum(j + tile0, 1), 0, 0)),
                pl.BlockSpec((9, f), lambda i, j: (0, 0)),
                pl.BlockSpec((1, f), lambda i, j: (0, 0)),
                pl.BlockSpec(w_down.shape, lambda i, j: (0, 0))]
    args = [gate, gate, gate, val, xs, mod, dw9, dw_b.reshape(1, f), w_down]
    if final:
        in_specs.append(pl.BlockSpec((1, d), lambda i, j: (0, 0)))
        args.append(final_g.reshape(1, d))
        out_shape = jax.ShapeDtypeStruct((bsz, s - tile0 * tm, d), F32)
        out_spec = pl.BlockSpec((1, tm, d), lambda i, j: (i, j, 0))
    else:
        out_shape = jax.ShapeDtypeStruct((bsz, s, d), F32)
        out_spec = tok(d)
    kern = functools.partial(_ffn_down_kernel, tile0=tile0, n_tiles_all=n_all, final=final, mod_idx=5)
    return pl.pallas_call(
        kern, out_shape=out_shape, grid=(bsz, n_all - tile0), in_specs=in_specs, out_specs=out_spec,
        scratch_shapes=[pltpu.VMEM((tm + 2 * gw + 16, f), F32), pltpu.VMEM((tm, f), BF16)],
        compiler_params=_cparams(("arbitrary", "arbitrary")), name="convglu_down",
    )(*args)


def _head_major_rows(w_cols, n_heads):
    d = w_cols.shape[0]
    w = w_cols.reshape(d, 4, n_heads).transpose(2, 1, 0)
    w = jnp.concatenate([w, jnp.zeros((n_heads, GATE_ROWS - 4, d), w.dtype)], axis=1)
    return w.reshape(n_heads * GATE_ROWS, d)


def _head_major_vec(b_cols, n_heads):
    b = b_cols.reshape(4, n_heads).T
    b = jnp.concatenate([b, jnp.zeros((n_heads, GATE_ROWS - 4), b.dtype)], axis=1)
    return b.reshape(n_heads * GATE_ROWS)


def kernel(x, c, ctx, c_ctx, ada_w, ada_b, norm1_g, norm2_g, e_w_in, e_b_in, e_head_g, e_conf_dw, e_conf_dw_b, e_conf_ln_g, e_conf_ln_b, e_w_out, o_w_in, o_short_w, o_a_log, o_dt_bias, o_head_g, o_w_out, f_w_up, f_dw, f_dw_b, f_w_down, final_g):
    bsz, t, d = x.shape
    sc = ctx.shape[1]
    depth = ada_w.shape[0]
    assert sc == TOKEN_TILE and t % TOKEN_TILE == 0 and t % CHUNK == 0 and sc % CHUNK == 0
    s = sc + t
    n_ctx, n_all = sc // CHUNK, s // CHUNK
    mw = MLSTM_HEADS * LANES
    gwd = GDN_HEADS * LANES
    ffn = f_w_down.shape[1]

    xs = jnp.concatenate([ctx, x], axis=1)

    rows = 16
    c_rows = jnp.concatenate([c, c_ctx[None, :], jnp.zeros((rows - bsz - 1, d), F32)], axis=0)
    mod_all = _ada_mod(c_rows, ada_w, ada_b)

    for layer in range(depth):
        last = layer == depth - 1
        tile0 = 1 if last else 0
        ml = mod_all[layer]
        mod = jnp.stack([jnp.broadcast_to(ml[bsz][None], (bsz, 6 * d)), ml[:bsz]], axis=1)[:, :, None, :]
        jj = layer // 2
        if layer % 2 == 0:
            w_in, b_in = e_w_in[jj], e_b_in[jj]
            g0 = 4 * mw
            g1 = g0 + 4 * MLSTM_HEADS
            w_main = jnp.concatenate([w_in[:, :g0], w_in[:, g1:]], axis=1).astype(BF16)
            b_main = jnp.concatenate([b_in[:g0], b_in[g1:]])
            w_rows = _head_major_rows(w_in[:, g0:g1], MLSTM_HEADS).astype(BF16)
            b_rows = _head_major_vec(b_in[g0:g1], MLSTM_HEADS)
            qkvo, glu, gates_t = _proj(xs, mod, norm1_g[layer], w_main, b_main, mod_idx=0,
                                       seg_widths=(g0, w_in.shape[1] - g1), w_rows=w_rows, b_rows=b_rows)
            y = _mlstm(qkvo, gates_t, n_ctx=n_ctx, n_all=n_all)
            c_out = _conformer(glu, e_conf_dw[jj], e_conf_dw_b[jj], e_conf_ln_g[jj], e_conf_ln_b[jj], sc=sc)
            x1 = _finish(y, qkvo, 3, c_out, xs, mod, e_head_g[jj], e_w_out[jj].astype(BF16),
                         n_heads=MLSTM_HEADS, tile0=tile0)
        else:
            w_in = o_w_in[jj]
            g0 = 4 * gwd
            w_main = w_in[:, :g0].astype(BF16)
            w_rows = _head_major_rows(w_in[:, g0:], GDN_HEADS).astype(BF16)
            b_rows = jnp.zeros((GDN_HEADS * GATE_ROWS,), F32)
            qkv, z, ab_t = _proj(xs, mod, norm1_g[layer], w_main, jnp.zeros((g0,), F32), mod_idx=0,
                                 seg_widths=(3 * gwd, gwd), w_rows=w_rows, b_rows=b_rows)
            zpad = jnp.zeros((GDN_HEADS, GATE_ROWS - 3), F32)
            alog_rows = jnp.concatenate([o_a_log[jj][0][:, None], jnp.zeros((GDN_HEADS, 1), F32),
                                         o_a_log[jj][1][:, None], zpad], axis=1).reshape(-1, 1)
            dtb_rows = jnp.concatenate([o_dt_bias[jj][0][:, None], jnp.zeros((GDN_HEADS, 1), F32),
                                        o_dt_bias[jj][1][:, None], zpad], axis=1).reshape(-1, 1)
            y = _gdn(qkv, o_short_w[jj], ab_t, alog_rows, dtb_rows, n_ctx=n_ctx, n_all=n_all)
            x1 = _finish(y, z, 0, None, xs, mod, jnp.tile(o_head_g[jj], GDN_HEADS), o_w_out[jj].astype(BF16),
                         n_heads=GDN_HEADS, tile0=tile0)
        gate, val = _proj(x1, mod, norm2_g[layer], f_w_up[layer].astype(BF16), jnp.zeros((2 * ffn,), F32),
                          mod_idx=3, seg_widths=(ffn, ffn), tile0=tile0)
        xs = _ffn_down(gate, val, x1, mod, f_dw[layer].reshape(9, ffn), f_dw_b[layer],
                       f_w_down[layer].astype(BF16), tile0=tile0, final_g=final_g if last else None)
    return xs
```

```python
import functools

import jax
import jax.numpy as jnp
from jax import lax
from jax.experimental import pallas as pl
from jax.experimental.pallas import tpu as pltpu

F32 = jnp.float32
BF16 = jnp.bfloat16
EPS = 1e-6

LANES = 128
CHUNK = LANES
TOKEN_TILE = 256
GRID_W = 64
MLSTM_HEADS = 4
GDN_HEADS = 8
CONF_K = 31
SHORT_K = 5
GATE_ROWS = 8
SCAN_HEADS_PER_STEP = 4
FFN_CW = 256
VMEM_LIMIT = 56 * 1024 * 1024


def _cparams(sem):
    return pltpu.CompilerParams(dimension_semantics=sem, vmem_limit_bytes=VMEM_LIMIT)


def _sigmoid(x):
    return 1.0 / (1.0 + jnp.exp(-x))


def _silu(x):
    return x * _sigmoid(x)


def _softplus(x):
    return jnp.maximum(x, 0.0) + jnp.log1p(jnp.exp(-jnp.abs(x)))


def _log_sigmoid(x):
    return jnp.minimum(x, 0.0) - jnp.log1p(jnp.exp(-jnp.abs(x)))


def _mm(a, b):
    return jnp.dot(a.astype(BF16), b.astype(BF16), preferred_element_type=F32)


def _mm_nt(a, b):
    return lax.dot_general(a.astype(BF16), b.astype(BF16), (((1,), (1,)), ((), ())),
                           preferred_element_type=F32)


def _mm_tn(a, b):
    return lax.dot_general(a.astype(BF16), b.astype(BF16), (((0,), (0,)), ((), ())),
                           preferred_element_type=F32)


def _split3(x):
    hi = x.astype(BF16)
    r = x - hi.astype(F32)
    mid = r.astype(BF16)
    lo = (r - mid.astype(F32)).astype(BF16)
    return hi, mid, lo


def _mm_x01(x, m01):
    hi, mid, lo = _split3(x)
    dot = functools.partial(jnp.dot, preferred_element_type=F32)
    return dot(hi, m01) + dot(mid, m01) + dot(lo, m01)


def _chunk_masks(L):
    row = lax.broadcasted_iota(jnp.int32, (L, L), 0)
    col = lax.broadcasted_iota(jnp.int32, (L, L), 1)
    return row, col


def _ada_kernel(c_ref, w_ref, b_ref, o_ref):
    c = c_ref[...]
    o_ref[0] = _mm(_silu(c), w_ref[0]) + b_ref[0]


def _ada_mod(c_rows, ada_w, ada_b):
    depth, d, n = ada_w.shape
    tn = 1536
    assert n % tn == 0
    rows = c_rows.shape[0]
    return pl.pallas_call(
        _ada_kernel,
        out_shape=jax.ShapeDtypeStruct((depth, rows, n), F32),
        grid=(depth, n // tn),
        in_specs=[pl.BlockSpec((rows, d), lambda l, j: (0, 0)),
                  pl.BlockSpec((1, d, tn), lambda l, j: (l, 0, j)),
                  pl.BlockSpec((1, 1, tn), lambda l, j: (l, 0, j))],
        out_specs=pl.BlockSpec((1, rows, tn), lambda l, j: (l, 0, j)),
        compiler_params=_cparams(("arbitrary", "arbitrary")),
        name="ada_mod",
    )(c_rows, ada_w, ada_b.reshape(depth, 1, n))


def _proj_kernel(x_ref, mod_ref, g_ref, w_ref, b_ref, *rest, mod_idx, segs, with_rows, tn):
    d = x_ref.shape[-1]
    if with_rows:
        wr_ref, br_ref = rest[:2]
        outs = rest[2:]
    else:
        outs = rest
    x = x_ref[0]
    ms = jnp.mean(x * x, axis=-1, keepdims=True)
    r = x * lax.rsqrt(ms + EPS)
    mod = mod_ref[0, 0]
    shift = mod[:, mod_idx * d:(mod_idx + 1) * d]
    scale = mod[:, (mod_idx + 1) * d:(mod_idx + 2) * d]
    h = r * g_ref[...] * (1.0 + scale) + shift
    hb = h.astype(BF16)
    for (off, width), o_ref in zip(segs, outs):
        for n0 in range(0, width, tn):
            nw = min(tn, width - n0)
            acc = jnp.dot(hb, w_ref[:, off + n0:off + n0 + nw], preferred_element_type=F32)
            o_ref[0, :, n0:n0 + nw] = (acc + b_ref[:, off + n0:off + n0 + nw]).astype(o_ref.dtype)
    if with_rows:
        rows = lax.dot_general(wr_ref[...], hb, (((1,), (1,)), ((), ())), preferred_element_type=F32)
        outs[-1][0] = rows + br_ref[...]


def _proj(xs, mod, g, w, b, *, mod_idx, seg_widths, w_rows=None, b_rows=None, tile0=0):
    bsz, s, d = xs.shape
    tm = TOKEN_TILE
    n_tiles = s // tm - tile0
    segs, off = [], 0
    for wd in seg_widths:
        segs.append((off, wd))
        off += wd
    assert off == w.shape[1]
    with_rows = w_rows is not None
    in_specs = [pl.BlockSpec((1, tm, d), lambda i, j: (i, j + tile0, 0)),
                pl.BlockSpec((1, 1, 1, mod.shape[-1]), lambda i, j: (i, jnp.minimum(j + tile0, 1), 0, 0)),
                pl.BlockSpec((1, d), lambda i, j: (0, 0)),
                pl.BlockSpec(w.shape, lambda i, j: (0, 0)),
                pl.BlockSpec((1, w.shape[1]), lambda i, j: (0, 0))]
    args = [xs, mod, g.reshape(1, d), w, b.reshape(1, -1)]
    out_shape = [jax.ShapeDtypeStruct((bsz, s, wd), BF16) for wd in seg_widths]
    out_specs = [pl.BlockSpec((1, tm, wd), lambda i, j: (i, j + tile0, 0)) for wd in seg_widths]
    if with_rows:
        nr = w_rows.shape[0]
        in_specs += [pl.BlockSpec(w_rows.shape, lambda i, j: (0, 0)),
                     pl.BlockSpec((nr, 1), lambda i, j: (0, 0))]
        args += [w_rows, b_rows.reshape(nr, 1)]
        out_shape.append(jax.ShapeDtypeStruct((bsz, nr, s), F32))
        out_specs.append(pl.BlockSpec((1, nr, tm), lambda i, j: (i, 0, j + tile0)))
    kern = functools.partial(_proj_kernel, mod_idx=mod_idx, segs=tuple(segs), with_rows=with_rows, tn=512)
    return pl.pallas_call(
        kern, out_shape=out_shape, grid=(bsz, n_tiles), in_specs=in_specs, out_specs=out_specs,
        compiler_params=_cparams(("arbitrary", "arbitrary")), name="norm_proj",
    )(*args)


def _bwd_chunk(i, n_ctx, n_all):
    return jnp.where(i < n_ctx, n_ctx - 1 - i, n_all - 1 - (i - n_ctx))


def _col_bcast(masked_rows, ones_b):
    return _mm_x01(masked_rows, ones_b)


def _mlstm_kernel(q_ref, k_ref, v_ref, g_ref, y_ref, c_scr, *, n_ctx, n_all, hp):
    L = CHUNK
    row, col = _chunk_masks(L)
    incl = (col <= row, col >= row)
    csum_r = (incl[1].astype(BF16), incl[0].astype(BF16))
    eye = row == col
    ones_b = jnp.ones((L, LANES), BF16)
    scale = float(LANES) ** -0.5

    y_ref[...] = jnp.zeros(y_ref.shape, F32)
    c_scr[...] = jnp.zeros(c_scr.shape, F32)

    chains = [(h, d) for h in range(hp) for d in (0, 1)]
    lanes = lambda h: slice(h * LANES, (h + 1) * LANES)

    def body(i, carry):
        n0, m0 = carry
        r0 = (pl.multiple_of(i * L, L), pl.multiple_of(_bwd_chunk(i, n_ctx, n_all) * L, L))
        gt = [g_ref[0, :, pl.ds(r0[d], L)] for d in (0, 1)]
        lsr = [_log_sigmoid(g) for g in gt]
        bsum = [_mm_x01(lsr[d], csum_r[d]) for d in (0, 1)]
        row = lambda arr, h, d, kind: arr[d][h * GATE_ROWS + 2 * d + kind:h * GATE_ROWS + 2 * d + kind + 1, :]
        li_r = [row(gt, h, d, 0) for h, d in chains]
        lf_r = [row(lsr, h, d, 1) for h, d in chains]
        b_r = [row(bsum, h, d, 1) for h, d in chains]
        q = [q_ref[0, pl.ds(r0[d], L), lanes(h)].astype(F32) * scale for h, d in chains]
        k = [k_ref[0, pl.ds(r0[d], L), lanes(h)].astype(F32) for h, d in chains]
        v = [v_ref[0, pl.ds(r0[d], L), lanes(h)].astype(F32) for h, d in chains]
        b_c = [_col_bcast(jnp.where(incl[d], f, 0.0), ones_b) for (h, d), f in zip(chains, lf_r)]
        li_c = [_col_bcast(jnp.where(eye, x, 0.0), ones_b) for x in li_r]
        total = [b[:, L - 1:L] if d == 0 else b[:, 0:1] for (h, d), b in zip(chains, b_r)]
        c0 = [c_scr[2 * h + d] for h, d in chains]
        m_loc = [jnp.max(t - b + x, axis=1, keepdims=True) for t, b, x in zip(total, b_r, li_r)]
        kw = [kx * jnp.exp(t - b + x - m) for kx, t, b, x, m in zip(k, total, b_c, li_c, m_loc)]
        c_loc = [_mm_tn(a, vv) for a, vv in zip(kw, v)]
        m_new = [jnp.maximum(t + m, ml) for t, m, ml in zip(total, m0, m_loc)]
        a_old = [jnp.exp(t + m - mn) for t, m, mn in zip(total, m0, m_new)]
        a_loc = [jnp.exp(ml - mn) for ml, mn in zip(m_loc, m_new)]
        for (h, d), ao, c, al, cl in zip(chains, a_old, c0, a_loc, c_loc):
            c_scr[2 * h + d] = ao * c + al * cl
        n_new = [ao * n + al * jnp.sum(a, axis=0, keepdims=True)
                 for ao, n, al, a in zip(a_old, n0, a_loc, kw)]
        dlog = [jnp.where(incl[d], bc - br + x, -jnp.inf) for (h, d), bc, br, x in zip(chains, b_c, b_r, li_r)]
        inter = [bc + m for bc, m in zip(b_c, m0)]
        m_t = [jnp.maximum(x, jnp.max(dl, axis=1, keepdims=True)) for x, dl in zip(inter, dlog)]
        s = [_mm_nt(qq, kx) for qq, kx in zip(q, k)]
        p = [jnp.exp(dl - mt) * ss for dl, mt, ss in zip(dlog, m_t, s)]
        a_int = [jnp.exp(x - mt) for x, mt in zip(inter, m_t)]
        num = [ai * _mm(qq, c) + _mm(pp, vv) for ai, qq, c, pp, vv in zip(a_int, q, c0, p, v)]
        den = [ai * jnp.sum(qq * n, axis=1, keepdims=True) + jnp.sum(pp, axis=1, keepdims=True)
               for ai, qq, n, pp in zip(a_int, q, n0, p)]
        for (h, d), nu, de, mt in zip(chains, num, den, m_t):
            y_ref[0, pl.ds(r0[d], L), lanes(h)] += nu / jnp.maximum(jnp.abs(de), jnp.exp(-mt))
        return tuple(n_new), tuple(m_new)

    zn = tuple(jnp.zeros((1, LANES), F32) for _ in chains)
    zm = tuple(jnp.zeros((1, 1), F32) for _ in chains)
    lax.fori_loop(0, n_all, body, (zn, zm))


def _mlstm(qkvo, gates_t, *, n_ctx, n_all):
    bsz, s, _ = qkvo.shape
    h = MLSTM_HEADS
    hp = SCAN_HEADS_PER_STEP
    ng = h // hp
    wd = hp * LANES
    kern = functools.partial(_mlstm_kernel, n_ctx=n_ctx, n_all=n_all, hp=hp)
    blk = lambda off: pl.BlockSpec((1, s, wd), lambda i, j: (i, 0, j + off))
    return pl.pallas_call(
        kern, out_shape=jax.ShapeDtypeStruct((bsz, s, h * LANES), F32), grid=(bsz, ng),
        in_specs=[blk(0), blk(ng), blk(2 * ng),
                  pl.BlockSpec((1, hp * GATE_ROWS, s), lambda i, j: (i, j, 0))],
        out_specs=pl.BlockSpec((1, s, wd), lambda i, j: (i, 0, j)),
        scratch_shapes=[pltpu.VMEM((2 * hp, LANES, LANES), F32)],
        compiler_params=_cparams(("arbitrary", "arbitrary")), name="mlstm_scan",
    )(qkvo, qkvo, qkvo, gates_t)


def _gdn_kernel(q_ref, k_ref, v_ref, wq_ref, wk_ref, wv_ref, ab_ref, alog_ref, dtb_ref, y_ref,
                xp_scr, q_scr, k_scr, v_scr, s_scr, *, n_ctx, n_all, hp):
    L = CHUNK
    sc = n_ctx * L
    s_len = n_all * L
    row, col = _chunk_masks(L)
    incl = (col <= row, col >= row)
    strict = (col < row, col > row)
    csum_r = (incl[1].astype(BF16), incl[0].astype(BF16))
    eye = row == col
    eye_f = jnp.where(eye, 1.0, 0.0)
    blk = tuple(jnp.right_shift(row, sh) == jnp.right_shift(col, sh) for sh in (4, 5, 6, 7))
    ones_b = jnp.ones((L, LANES), BF16)
    qscale = float(LANES) ** -0.5
    pad = 8

    width = hp * LANES
    zeros_pad = jnp.zeros((pad, width), F32)
    xp_scr[0:pad, :] = zeros_pad
    xp_scr[pad + sc:2 * pad + sc, :] = zeros_pad
    xp_scr[2 * pad + s_len:3 * pad + s_len, :] = zeros_pad
    half = SHORT_K // 2
    for src, w_ref, dst, norm in ((q_ref, wq_ref, q_scr, True), (k_ref, wk_ref, k_scr, True),
                                  (v_ref, wv_ref, v_scr, False)):
        xp_scr[pad:pad + sc, :] = src[0, 0:sc, :].astype(F32)
        xp_scr[2 * pad + sc:2 * pad + s_len, :] = src[0, sc:s_len, :].astype(F32)
        for c in range(n_all):
            base = c * L + (pad if c < n_ctx else 2 * pad)
            acc = jnp.zeros((L, width), F32)
            for j in range(SHORT_K):
                acc = acc + w_ref[j:j + 1, :] * xp_scr[base + j - half:base + j - half + L, :]
            a = _silu(acc)
            for h in range(hp):
                ah = a[:, h * LANES:(h + 1) * LANES]
                if norm:
                    ah = ah * lax.rsqrt(jnp.sum(ah * ah, axis=-1, keepdims=True) + EPS)
                dst[c * L:(c + 1) * L, h * LANES:(h + 1) * LANES] = ah

    y_ref[...] = jnp.zeros(y_ref.shape, F32)
    s_scr[...] = jnp.zeros(s_scr.shape, F32)
    arate = jnp.exp(alog_ref[...])
    dtb = dtb_ref[...]

    chains = [(h, d) for h in range(hp) for d in (0, 1)]
    off_masks = [jnp.logical_and(blk[lvl], jnp.logical_not(blk[lvl - 1])) for lvl in range(1, len(blk))]

    def body(i, carry):
        r0 = (pl.multiple_of(i * L, L), pl.multiple_of(_bwd_chunk(i, n_ctx, n_all) * L, L))
        ab = [ab_ref[0, :, pl.ds(r0[d], L)] for d in (0, 1)]
        g8 = [-arate * _softplus(a + dtb) for a in ab]
        beta8 = [_sigmoid(a) for a in ab]
        gcs = [_mm_x01(g8[d], csum_r[d]) for d in (0, 1)]
        row = lambda arr, h, d, kind: arr[d][h * GATE_ROWS + 2 * d + kind:h * GATE_ROWS + 2 * d + kind + 1, :]
        g_r = [row(g8, h, d, 0) for h, d in chains]
        gc_r = [row(gcs, h, d, 0) for h, d in chains]
        beta_r = [row(beta8, h, d, 1) for h, d in chains]
        lanes = lambda h: slice(h * LANES, (h + 1) * LANES)
        q = [q_scr[pl.ds(r0[d], L), lanes(h)] * qscale for h, d in chains]
        k = [k_scr[pl.ds(r0[d], L), lanes(h)] for h, d in chains]
        v = [v_scr[pl.ds(r0[d], L), lanes(h)] for h, d in chains]
        gc_c = [_col_bcast(jnp.where(incl[d], g, 0.0), ones_b) for (h, d), g in zip(chains, g_r)]
        beta_c = [_col_bcast(jnp.where(eye, b, 0.0), ones_b) for b in beta_r]
        total = [g[:, L - 1:L] if d == 0 else g[:, 0:1] for (h, d), g in zip(chains, gc_r)]
        dec = [jnp.exp(jnp.where(incl[d], c - r, -jnp.inf)) for (h, d), c, r in zip(chains, gc_c, gc_r)]
        kk = [_mm_nt(x, x) for x in k]
        a_mat = [jnp.where(strict[d], b * m * e, 0.0) for (h, d), b, m, e in zip(chains, beta_c, kk, dec)]
        egc = [jnp.exp(c) for c in gc_c]
        rhs = [jnp.concatenate([b * vv, (b * e) * kx], axis=1)
               for b, vv, e, kx in zip(beta_c, v, egc, k)]
        pw = [-jnp.where(blk[0], a, 0.0) for a in a_mat]
        t_inv = [eye_f + p for p in pw]
        for _ in range(3):
            pw = [_mm(p, p) for p in pw]
            t_inv = [t + _mm(t, p) for t, p in zip(t_inv, pw)]
        for om in off_masks:
            ta = [_mm(t, jnp.where(om, a, 0.0)) for t, a in zip(t_inv, a_mat)]
            t_inv = [t - _mm(x, t) for t, x in zip(t_inv, ta)]
        sol = [_mm(t, x) for t, x in zip(t_inv, rhs)]
        s0 = [s_scr[2 * h + d] for h, d in chains]
        v_new = [x[:, :LANES] - _mm(x[:, LANES:], s) for x, s in zip(sol, s0)]
        kdec = [jnp.exp(t - c) * kx for t, c, kx in zip(total, gc_c, k)]
        upd = [_mm_tn(kd, vn) for kd, vn in zip(kdec, v_new)]
        for (h, d), t, s, u in zip(chains, total, s0, upd):
            s_scr[2 * h + d] = jnp.exp(t) * s + u
        qk = [_mm_nt(qq, kx) * e for qq, kx, e in zip(q, k, dec)]
        o = [_mm(e * qq, s) + _mm(m, vn) for e, qq, s, m, vn in zip(egc, q, s0, qk, v_new)]
        for (h, d), oo in zip(chains, o):
            y_ref[0, pl.ds(r0[d], L), lanes(h)] += oo
        return carry

    lax.fori_loop(0, n_all, body, 0)


def _gdn(qkv, short_w, ab_t, alog_rows, dtb_rows, *, n_ctx, n_all):
    bsz, s, _ = qkv.shape
    h = GDN_HEADS
    hp = SCAN_HEADS_PER_STEP
    ng = h // hp
    wd = hp * LANES
    kern = functools.partial(_gdn_kernel, n_ctx=n_ctx, n_all=n_all, hp=hp)
    blk = lambda off: pl.BlockSpec((1, s, wd), lambda i, j: (i, 0, j + off))
    wblk = lambda off: pl.BlockSpec((SHORT_K, wd), lambda i, j: (0, j + off))
    return pl.pallas_call(
        kern, out_shape=jax.ShapeDtypeStruct((bsz, s, h * LANES), F32), grid=(bsz, ng),
        in_specs=[blk(0), blk(ng), blk(2 * ng), wblk(0), wblk(ng), wblk(2 * ng),
                  pl.BlockSpec((1, hp * GATE_ROWS, s), lambda i, j: (i, j, 0)),
                  pl.BlockSpec((hp * GATE_ROWS, 1), lambda i, j: (j, 0)),
                  pl.BlockSpec((hp * GATE_ROWS, 1), lambda i, j: (j, 0))],
        out_specs=pl.BlockSpec((1, s, wd), lambda i, j: (i, 0, j)),
        scratch_shapes=[pltpu.VMEM((s + 24, wd), F32), pltpu.VMEM((s, wd), F32),
                        pltpu.VMEM((s, wd), F32), pltpu.VMEM((s, wd), F32),
                        pltpu.VMEM((2 * hp, LANES, LANES), F32)],
        compiler_params=_cparams(("arbitrary", "arbitrary")), name="gdn_scan",
    )(qkv, qkv, qkv, short_w, short_w, short_w, ab_t, alog_rows, dtb_rows)


def _conformer_kernel(glu_ref, w_ref, b_ref, lg_ref, lb_ref, o_ref, u_scr, *, sc, s_len):
    ch = o_ref.shape[-1]
    pad = 16
    rt = 32
    half = CONF_K // 2
    zeros_pad = jnp.zeros((pad, ch), F32)
    u_scr[0:pad, :] = zeros_pad
    u_scr[pad + sc:2 * pad + sc, :] = zeros_pad
    u_scr[2 * pad + s_len:3 * pad + s_len, :] = zeros_pad

    def fill(i, carry):
        r0 = pl.multiple_of(i * rt, rt)
        off = jnp.where(r0 < sc, pad, 2 * pad)
        a = glu_ref[0, pl.ds(r0, rt), 0:ch].astype(F32)
        gt = glu_ref[0, pl.ds(r0, rt), ch:2 * ch].astype(F32)
        u_scr[pl.ds(pl.multiple_of(r0 + off, 8), rt), :] = a * _sigmoid(gt)
        return carry

    lax.fori_loop(0, s_len // rt, fill, 0)

    def conv(i, carry):
        r0 = pl.multiple_of(i * rt, rt)
        off = jnp.where(r0 < sc, pad, 2 * pad)
        win = u_scr[pl.ds(pl.multiple_of(r0 + off - pad, 8), rt + 2 * pad), :]
        nwin = rt + 2 * pad
        acc = jnp.zeros((rt, ch), F32)
        for res in range(8):
            shifted = win if res == 0 else pltpu.roll(win, nwin - res, axis=0)
            for a8 in range(0, nwin - rt + 1, 8):
                j = a8 + res - (pad - half)
                if 0 <= j < CONF_K:
                    acc = acc + w_ref[j:j + 1, :] * shifted[a8:a8 + rt, :]
        y = acc + b_ref[...]
        mu = jnp.mean(y, axis=-1, keepdims=True)
        yc = y - mu
        var = jnp.mean(yc * yc, axis=-1, keepdims=True)
        z = yc * lax.rsqrt(var + EPS) * lg_ref[...] + lb_ref[...]
        o_ref[0, pl.ds(r0, rt), :] = _silu(z).astype(o_ref.dtype)
        return carry

    lax.fori_loop(0, s_len // rt, conv, 0)


def _conformer(glu, dw, dw_b, ln_g, ln_b, *, sc):
    bsz, s, c2 = glu.shape
    ch = c2 // 2
    kern = functools.partial(_conformer_kernel, sc=sc, s_len=s)
    vec = lambda: pl.BlockSpec((1, ch), lambda i: (0, 0))
    return pl.pallas_call(
        kern, out_shape=jax.ShapeDtypeStruct((bsz, s, ch), BF16), grid=(bsz,),
        in_specs=[pl.BlockSpec((1, s, c2), lambda i: (i, 0, 0)),
                  pl.BlockSpec((CONF_K, ch), lambda i: (0, 0)), vec(), vec(), vec()],
        out_specs=pl.BlockSpec((1, s, ch), lambda i: (i, 0, 0)),
        scratch_shapes=[pltpu.VMEM((s + 48, ch), F32)],
        compiler_params=_cparams(("arbitrary",)), name="conformer_conv",
    )(glu, dw, dw_b.reshape(1, ch), ln_g.reshape(1, ch), ln_b.reshape(1, ch))


def _finish_kernel(y_ref, gate_ref, *rest, n_heads, even, mod_idx):
    if even:
        c_ref, x_ref, mod_ref, hg_ref, w_ref, o_ref = rest
    else:
        x_ref, mod_ref, hg_ref, w_ref, o_ref = rest
    d = x_ref.shape[-1]
    y = y_ref[0]
    parts = []
    for h in range(n_heads):
        yh = y[:, h * LANES:(h + 1) * LANES]
        ms = jnp.mean(yh * yh, axis=-1, keepdims=True)
        parts.append(yh * lax.rsqrt(ms + EPS))
    yn = jnp.concatenate(parts, axis=1)
    gate = gate_ref[0].astype(F32)
    if even:
        m = yn * hg_ref[...] * _sigmoid(gate)
        cat = jnp.concatenate([m.astype(BF16), c_ref[0].astype(BF16)], axis=1)
    else:
        cat = ((yn * hg_ref[...]) * _silu(gate)).astype(BF16)
    out = jnp.dot(cat, w_ref[...], preferred_element_type=F32)
    g1 = mod_ref[0, 0][:, mod_idx * d:(mod_idx + 1) * d]
    o_ref[0] = x_ref[0] + g1 * out


def _finish(y, gate_src, gate_blk, c_out, xs, mod, head_g, w_out, *, n_heads, tile0=0):
    bsz, s, d = xs.shape
    tm = TOKEN_TILE
    wdt = n_heads * LANES
    even = c_out is not None
    tok = lambda width, blk=0: pl.BlockSpec((1, tm, width), lambda i, j: (i, j + tile0, blk))
    in_specs = [tok(wdt), tok(wdt, gate_blk)]
    args = [y, gate_src]
    if even:
        in_specs.append(tok(c_out.shape[-1]))
        args.append(c_out)
    in_specs += [tok(d),
                 pl.BlockSpec((1, 1, 1, mod.shape[-1]), lambda i, j: (i, jnp.minimum(j + tile0, 1), 0, 0)),
                 pl.BlockSpec((1, wdt), lambda i, j: (0, 0)),
                 pl.BlockSpec(w_out.shape, lambda i, j: (0, 0))]
    args += [xs, mod, head_g.reshape(1, wdt), w_out]
    kern = functools.partial(_finish_kernel, n_heads=n_heads, even=even, mod_idx=2)
    return pl.pallas_call(
        kern, out_shape=jax.ShapeDtypeStruct((bsz, s, d), F32), grid=(bsz, s // tm - tile0),
        in_specs=in_specs, out_specs=tok(d),
        compiler_params=_cparams(("arbitrary", "arbitrary")), name="mixer_out",
    )(*args)


def _ffn_down_kernel(gm_ref, gp_ref, gn_ref, val_ref, x_ref, mod_ref, dw_ref, dwb_ref, wd_ref, *rest,
                     tile0, n_tiles_all, final, mod_idx):
    if final:
        fg_ref, o_ref, ext_scr, left_scr, right_scr, act_scr = rest
    else:
        o_ref, ext_scr, left_scr, right_scr, act_scr = rest
    tm = TOKEN_TILE
    gw = GRID_W
    f = gm_ref.shape[-1]
    d = x_ref.shape[-1]
    j = pl.program_id(1) + tile0
    lat = jnp.where(j > 0, 1.0, 0.0)
    jrow = jnp.broadcast_to(j, (gw, 1))
    has_prev = jrow >= 2
    has_next = jnp.logical_and(jrow >= 1, jrow <= n_tiles_all - 2)
    shifts = {}
    for n in (gw, tm):
        t = lax.broadcasted_iota(jnp.int32, (n, n), 0)
        u = lax.broadcasted_iota(jnp.int32, (n, n), 1)
        col = jnp.bitwise_and(t, gw - 1)
        ctx_seq = jnp.broadcast_to(j, (n, n)) == 0
        take_left = jnp.logical_and(u == t - 1, jnp.logical_or(col != 0, ctx_seq))
        take_right = jnp.logical_and(u == t + 1, jnp.logical_or(col != gw - 1, ctx_seq))
        shifts[n] = (jnp.where(take_left, 1.0, 0.0).astype(BF16), jnp.where(take_right, 1.0, 0.0).astype(BF16))

    for n0 in range(0, f, FFN_CW):
        cols = slice(n0, n0 + FFN_CW)
        blocks = ((0, gw, jnp.where(has_prev, gp_ref[0, :, cols], 0.0).astype(BF16)),
                  (gw, tm, gm_ref[0, :, cols]),
                  (gw + tm, gw, jnp.where(has_next, gn_ref[0, :, cols], 0.0).astype(BF16)))
        slot = (n0 // FFN_CW) % 2
        for r0, n, g in blocks:
            ext_scr[slot, r0:r0 + n, :] = g.astype(F32)
            left_scr[slot, r0:r0 + n, :] = jnp.dot(shifts[n][0], g, preferred_element_type=F32)
            right_scr[slot, r0:r0 + n, :] = jnp.dot(shifts[n][1], g, preferred_element_type=F32)
        for rs in range(tm // gw):
            acc = jnp.zeros((gw, FFN_CW), F32)
            for dr in range(3):
                base = (rs + dr) * gw
                for dc, src in enumerate((left_scr, ext_scr, right_scr)):
                    wv = dw_ref[dr * 3 + dc:dr * 3 + dc + 1, cols]
                    if dr != 1:
                        wv = wv * lat
                    acc = acc + src[slot, base:base + gw, :] * wv
            gate = acc + dwb_ref[:, cols]
            a = _silu(gate) * val_ref[0, rs * gw:(rs + 1) * gw, cols]
            act_scr[rs * gw:(rs + 1) * gw, cols] = a.astype(BF16)

    out = jnp.dot(act_scr[...], wd_ref[...], preferred_element_type=F32)
    g2 = mod_ref[0, 0][:, mod_idx * d:(mod_idx + 1) * d]
    x2 = x_ref[0] + g2 * out
    if final:
        ms = jnp.mean(x2 * x2, axis=-1, keepdims=True)
        x2 = x2 * lax.rsqrt(ms + EPS) * fg_ref[...]
    o_ref[0] = x2


def _ffn_down(gate, val, xs, mod, dw9, dw_b, w_down, *, tile0=0, final_g=None):
    bsz, s, f = gate.shape
    d = xs.shape[-1]
    tm = TOKEN_TILE
    gw = GRID_W
    n_all = s // tm
    per = tm // gw
    n_rows = s // gw
    final = final_g is not None
    tok = lambda width: pl.BlockSpec((1, tm, width), lambda i, j: (i, j + tile0, 0))
    in_specs = [tok(f),
                pl.BlockSpec((1, gw, f), lambda i, j: (i, jnp.maximum((j + tile0) * per - 1, tile0 * per), 0)),
                pl.BlockSpec((1, gw, f), lambda i, j: (i, jnp.minimum((j + tile0 + 1) * per, n_rows - 1), 0)),
                tok(f), tok(d),
                pl.BlockSpec((1, 1, 1, mod.shape[-1]), lambda i, j: (i, jnp.minimum(j + tile0, 1), 0, 0)),
                pl.BlockSpec((9, f), lambda i, j: (0, 0)),
                pl.BlockSpec((1, f), lambda i, j: (0, 0)),
                pl.BlockSpec(w_down.shape, lambda i, j: (0, 0))]
    args = [gate, gate, gate, val, xs, mod, dw9, dw_b.reshape(1, f), w_down]
    if final:
        in_specs.append(pl.BlockSpec((1, d), lambda i, j: (0, 0)))
        args.append(final_g.reshape(1, d))
        out_shape = jax.ShapeDtypeStruct((bsz, s - tile0 * tm, d), F32)
        out_spec = pl.BlockSpec((1, tm, d), lambda i, j: (i, j, 0))
    else:
        out_shape = jax.ShapeDtypeStruct((bsz, s, d), F32)
        out_spec = tok(d)
    kern = functools.partial(_ffn_down_kernel, tile0=tile0, n_tiles_all=n_all, final=final, mod_idx=5)
    return pl.pallas_call(
        kern, out_shape=out_shape, grid=(bsz, n_all - tile0), in_specs=in_specs, out_specs=out_spec,
        scratch_shapes=[pltpu.VMEM((2, tm + 2 * gw, FFN_CW), F32)] * 3 + [pltpu.VMEM((tm, f), BF16)],
        compiler_params=_cparams(("arbitrary", "arbitrary")), name="convglu_down",
    )(*args)


def _head_major_rows(w_cols, n_heads):
    d = w_cols.shape[0]
    w = w_cols.reshape(d, 4, n_heads).transpose(2, 1, 0)
    w = jnp.concatenate([w, jnp.zeros((n_heads, GATE_ROWS - 4, d), w.dtype)], axis=1)
    return w.reshape(n_heads * GATE_ROWS, d)


def _head_major_vec(b_cols, n_heads):
    b = b_cols.reshape(4, n_heads).T
    b = jnp.concatenate([b, jnp.zeros((n_heads, GATE_ROWS - 4), b.dtype)], axis=1)
    return b.reshape(n_heads * GATE_ROWS)


def kernel(x, c, ctx, c_ctx, ada_w, ada_b, norm1_g, norm2_g, e_w_in, e_b_in, e_head_g, e_conf_dw, e_conf_dw_b, e_conf_ln_g, e_conf_ln_b, e_w_out, o_w_in, o_short_w, o_a_log, o_dt_bias, o_head_g, o_w_out, f_w_up, f_dw, f_dw_b, f_w_down, final_g):
    bsz, t, d = x.shape
    sc = ctx.shape[1]
    depth = ada_w.shape[0]
    assert sc == TOKEN_TILE and t % TOKEN_TILE == 0 and t % CHUNK == 0 and sc % CHUNK == 0
    s = sc + t
    n_ctx, n_all = sc // CHUNK, s // CHUNK
    mw = MLSTM_HEADS * LANES
    gwd = GDN_HEADS * LANES
    ffn = f_w_down.shape[1]

    xs = jnp.concatenate([ctx, x], axis=1)

    rows = 16
    c_rows = jnp.concatenate([c, c_ctx[None, :], jnp.zeros((rows - bsz - 1, d), F32)], axis=0)
    mod_all = _ada_mod(c_rows, ada_w, ada_b)

    for layer in range(depth):
        last = layer == depth - 1
        tile0 = 1 if last else 0
        ml = mod_all[layer]
        mod = jnp.stack([jnp.broadcast_to(ml[bsz][None], (bsz, 6 * d)), ml[:bsz]], axis=1)[:, :, None, :]
        jj = layer // 2
        if layer % 2 == 0:
            w_in, b_in = e_w_in[jj], e_b_in[jj]
            g0 = 4 * mw
            g1 = g0 + 4 * MLSTM_HEADS
            w_main = jnp.concatenate([w_in[:, :g0], w_in[:, g1:]], axis=1).astype(BF16)
            b_main = jnp.concatenate([b_in[:g0], b_in[g1:]])
            w_rows = _head_major_rows(w_in[:, g0:g1], MLSTM_HEADS).astype(BF16)
            b_rows = _head_major_vec(b_in[g0:g1], MLSTM_HEADS)
            qkvo, glu, gates_t = _proj(xs, mod, norm1_g[layer], w_main, b_main, mod_idx=0,
                                       seg_widths=(g0, w_in.shape[1] - g1), w_rows=w_rows, b_rows=b_rows)
            y = _mlstm(qkvo, gates_t, n_ctx=n_ctx, n_all=n_all)
            c_out = _conformer(glu, e_conf_dw[jj], e_conf_dw_b[jj], e_conf_ln_g[jj], e_conf_ln_b[jj], sc=sc)
            x1 = _finish(y, qkvo, 3, c_out, xs, mod, e_head_g[jj], e_w_out[jj].astype(BF16),
                         n_heads=MLSTM_HEADS, tile0=tile0)
        else:
            w_in = o_w_in[jj]
            g0 = 4 * gwd
            w_main = w_in[:, :g0].astype(BF16)
            w_rows = _head_major_rows(w_in[:, g0:], GDN_HEADS).astype(BF16)
            b_rows = jnp.zeros((GDN_HEADS * GATE_ROWS,), F32)
            qkv, z, ab_t = _proj(xs, mod, norm1_g[layer], w_main, jnp.zeros((g0,), F32), mod_idx=0,
                                 seg_widths=(3 * gwd, gwd), w_rows=w_rows, b_rows=b_rows)
            zpad = jnp.zeros((GDN_HEADS, GATE_ROWS - 3), F32)
            alog_rows = jnp.concatenate([o_a_log[jj][0][:, None], jnp.zeros((GDN_HEADS, 1), F32),
                                         o_a_log[jj][1][:, None], zpad], axis=1).reshape(-1, 1)
            dtb_rows = jnp.concatenate([o_dt_bias[jj][0][:, None], jnp.zeros((GDN_HEADS, 1), F32),
                                        o_dt_bias[jj][1][:, None], zpad], axis=1).reshape(-1, 1)
            y = _gdn(qkv, o_short_w[jj], ab_t, alog_rows, dtb_rows, n_ctx=n_ctx, n_all=n_all)
            x1 = _finish(y, z, 0, None, xs, mod, jnp.tile(o_head_g[jj], GDN_HEADS), o_w_out[jj].astype(BF16),
                         n_heads=GDN_HEADS, tile0=tile0)
        gate, val = _proj(x1, mod, norm2_g[layer], f_w_up[layer].astype(BF16), jnp.zeros((2 * ffn,), F32),
                          mod_idx=3, seg_widths=(ffn, ffn), tile0=tile0)
        xs = _ffn_down(gate, val, x1, mod, f_dw[layer].reshape(9, ffn), f_dw_b[layer],
                       f_w_down[layer].astype(BF16), tile0=tile0, final_g=final_g if last else None)
    return xs
```

```python
import functools

import jax
import jax.numpy as jnp
from jax import lax
from jax.experimental import pallas as pl
from jax.experimental.pallas import tpu as pltpu

F32 = jnp.float32
BF16 = jnp.bfloat16
EPS = 1e-6

LANES = 128
CHUNK = LANES
TOKEN_TILE = 256
GRID_W = 64
MLSTM_HEADS = 4
GDN_HEADS = 8
CONF_K = 31
SHORT_K = 5
GATE_ROWS = 8
SCAN_HEADS_PER_STEP = 4
FFN_CW = 256
VMEM_LIMIT = 56 * 1024 * 1024


def _cparams(sem):
    return pltpu.CompilerParams(dimension_semantics=sem, vmem_limit_bytes=VMEM_LIMIT)


def _sigmoid(x):
    return 1.0 / (1.0 + jnp.exp(-x))


def _silu(x):
    return x * _sigmoid(x)


def _softplus(x):
    return jnp.maximum(x, 0.0) + jnp.log1p(jnp.exp(-jnp.abs(x)))


def _log_sigmoid(x):
    return jnp.minimum(x, 0.0) - jnp.log1p(jnp.exp(-jnp.abs(x)))


def _mm(a, b):
    return jnp.dot(a.astype(BF16), b.astype(BF16), preferred_element_type=F32)


def _mm_nt(a, b):
    return lax.dot_general(a.astype(BF16), b.astype(BF16), (((1,), (1,)), ((), ())),
                           preferred_element_type=F32)


def _mm_tn(a, b):
    return lax.dot_general(a.astype(BF16), b.astype(BF16), (((0,), (0,)), ((), ())),
                           preferred_element_type=F32)


def _split3(x):
    hi = x.astype(BF16)
    r = x - hi.astype(F32)
    mid = r.astype(BF16)
    lo = (r - mid.astype(F32)).astype(BF16)
    return hi, mid, lo


def _mm_x01(x, m01):
    hi, mid, lo = _split3(x)
    dot = functools.partial(jnp.dot, preferred_element_type=F32)
    return dot(hi, m01) + dot(mid, m01) + dot(lo, m01)


def _chunk_masks(L):
    row = lax.broadcasted_iota(jnp.int32, (L, L), 0)
    col = lax.broadcasted_iota(jnp.int32, (L, L), 1)
    return row, col


def _ada_kernel(c_ref, w_ref, b_ref, o_ref):
    c = c_ref[...]
    o_ref[0] = _mm(_silu(c), w_ref[0]) + b_ref[0]


def _ada_mod(c_rows, ada_w, ada_b):
    depth, d, n = ada_w.shape
    tn = 1536
    assert n % tn == 0
    rows = c_rows.shape[0]
    return pl.pallas_call(
        _ada_kernel,
        out_shape=jax.ShapeDtypeStruct((depth, rows, n), F32),
        grid=(depth, n // tn),
        in_specs=[pl.BlockSpec((rows, d), lambda l, j: (0, 0)),
                  pl.BlockSpec((1, d, tn), lambda l, j: (l, 0, j)),
                  pl.BlockSpec((1, 1, tn), lambda l, j: (l, 0, j))],
        out_specs=pl.BlockSpec((1, rows, tn), lambda l, j: (l, 0, j)),
        compiler_params=_cparams(("arbitrary", "arbitrary")),
        name="ada_mod",
    )(c_rows, ada_w, ada_b.reshape(depth, 1, n))


def _proj_kernel(x_ref, mod_ref, g_ref, w_ref, b_ref, *rest, mod_idx, segs, with_rows, tn):
    d = x_ref.shape[-1]
    if with_rows:
        wr_ref, br_ref = rest[:2]
        outs = rest[2:]
    else:
        outs = rest
    x = x_ref[0]
    ms = jnp.mean(x * x, axis=-1, keepdims=True)
    r = x * lax.rsqrt(ms + EPS)
    mod = mod_ref[0, 0]
    shift = mod[:, mod_idx * d:(mod_idx + 1) * d]
    scale = mod[:, (mod_idx + 1) * d:(mod_idx + 2) * d]
    h = r * g_ref[...] * (1.0 + scale) + shift
    hb = h.astype(BF16)
    for (off, width), o_ref in zip(segs, outs):
        for n0 in range(0, width, tn):
            nw = min(tn, width - n0)
            acc = jnp.dot(hb, w_ref[:, off + n0:off + n0 + nw], preferred_element_type=F32)
            o_ref[0, :, n0:n0 + nw] = (acc + b_ref[:, off + n0:off + n0 + nw]).astype(o_ref.dtype)
    if with_rows:
        rows = lax.dot_general(wr_ref[...], hb, (((1,), (1,)), ((), ())), preferred_element_type=F32)
        outs[-1][0] = rows + br_ref[...]


def _proj(xs, mod, g, w, b, *, mod_idx, seg_widths, w_rows=None, b_rows=None, tile0=0):
    bsz, s, d = xs.shape
    tm = TOKEN_TILE
    n_tiles = s // tm - tile0
    segs, off = [], 0
    for wd in seg_widths:
        segs.append((off, wd))
        off += wd
    assert off == w.shape[1]
    with_rows = w_rows is not None
    in_specs = [pl.BlockSpec((1, tm, d), lambda i, j: (i, j + tile0, 0)),
                pl.BlockSpec((1, 1, 1, mod.shape[-1]), lambda i, j: (i, jnp.minimum(j + tile0, 1), 0, 0)),
                pl.BlockSpec((1, d), lambda i, j: (0, 0)),
                pl.BlockSpec(w.shape, lambda i, j: (0, 0)),
                pl.BlockSpec((1, w.shape[1]), lambda i, j: (0, 0))]
    args = [xs, mod, g.reshape(1, d), w, b.reshape(1, -1)]
    out_shape = [jax.ShapeDtypeStruct((bsz, s, wd), BF16) for wd in seg_widths]
    out_specs = [pl.BlockSpec((1, tm, wd), lambda i, j: (i, j + tile0, 0)) for wd in seg_widths]
    if with_rows:
        nr = w_rows.shape[0]
        in_specs += [pl.BlockSpec(w_rows.shape, lambda i, j: (0, 0)),
                     pl.BlockSpec((nr, 1), lambda i, j: (0, 0))]
        args += [w_rows, b_rows.reshape(nr, 1)]
        out_shape.append(jax.ShapeDtypeStruct((bsz, nr, s), F32))
        out_specs.append(pl.BlockSpec((1, nr, tm), lambda i, j: (i, 0, j + tile0)))
    kern = functools.partial(_proj_kernel, mod_idx=mod_idx, segs=tuple(segs), with_rows=with_rows, tn=512)
    return pl.pallas_call(
        kern, out_shape=out_shape, grid=(bsz, n_tiles), in_specs=in_specs, out_specs=out_specs,
        compiler_params=_cparams(("arbitrary", "arbitrary")), name="norm_proj",
    )(*args)


def _bwd_chunk(i, n_ctx, n_all):
    return jnp.where(i < n_ctx, n_ctx - 1 - i, n_all - 1 - (i - n_ctx))


def _rows_to_columns(*row_blocks):
    n = row_blocks[0].shape[1]
    used = sum(b.shape[0] for b in row_blocks)
    stacked = jnp.concatenate(list(row_blocks) + [jnp.zeros((n - used, n), F32)], axis=0)
    return stacked.T


def _column(cols_t, r):
    return jnp.broadcast_to(cols_t[:, r:r + 1], (cols_t.shape[0], LANES))


def _mlstm_kernel(q_ref, k_ref, v_ref, g_ref, y_ref, c_scr, *, n_ctx, n_all, hp):
    L = CHUNK
    row, col = _chunk_masks(L)
    incl = (col <= row, col >= row)
    csum_r = (incl[1].astype(BF16), incl[0].astype(BF16))
    eye = row == col
    ones_b = jnp.ones((L, LANES), BF16)
    scale = float(LANES) ** -0.5

    y_ref[...] = jnp.zeros(y_ref.shape, F32)
    c_scr[...] = jnp.zeros(c_scr.shape, F32)

    chains = [(h, d) for h in range(hp) for d in (0, 1)]
    lanes = lambda h: slice(h * LANES, (h + 1) * LANES)

    def body(i, carry):
        n0, m0 = carry
        r0 = (pl.multiple_of(i * L, L), pl.multiple_of(_bwd_chunk(i, n_ctx, n_all) * L, L))
        gt = [g_ref[0, :, pl.ds(r0[d], L)] for d in (0, 1)]
        lsr = [_log_sigmoid(g) for g in gt]
        bsum = [_mm_x01(lsr[d], csum_r[d]) for d in (0, 1)]
        row = lambda arr, h, d, kind: arr[d][h * GATE_ROWS + 2 * d + kind:h * GATE_ROWS + 2 * d + kind + 1, :]
        li_r = [row(gt, h, d, 0) for h, d in chains]
        b_r = [row(bsum, h, d, 1) for h, d in chains]
        q = [q_ref[0, pl.ds(r0[d], L), lanes(h)].astype(F32) * scale for h, d in chains]
        k = [k_ref[0, pl.ds(r0[d], L), lanes(h)].astype(F32) for h, d in chains]
        v = [v_ref[0, pl.ds(r0[d], L), lanes(h)].astype(F32) for h, d in chains]
        lf_r = [row(lsr, h, d, 1) for h, d in chains]
        b_c = [_mm_x01(jnp.where(incl[d], f, 0.0), ones_b) for (h, d), f in zip(chains, lf_r)]
        li_c = [_mm_x01(jnp.where(eye, x, 0.0), ones_b) for x in li_r]
        total = [b[:, L - 1:L] if d == 0 else b[:, 0:1] for (h, d), b in zip(chains, b_r)]
        c0 = [c_scr[2 * h + d] for h, d in chains]
        m_loc = [jnp.max(t - b + x, axis=1, keepdims=True) for t, b, x in zip(total, b_r, li_r)]
        kw = [kx * jnp.exp(t - b + x - m) for kx, t, b, x, m in zip(k, total, b_c, li_c, m_loc)]
        c_loc = [_mm_tn(a, vv) for a, vv in zip(kw, v)]
        m_new = [jnp.maximum(t + m, ml) for t, m, ml in zip(total, m0, m_loc)]
        a_old = [jnp.exp(t + m - mn) for t, m, mn in zip(total, m0, m_new)]
        a_loc = [jnp.exp(ml - mn) for ml, mn in zip(m_loc, m_new)]
        for (h, d), ao, c, al, cl in zip(chains, a_old, c0, a_loc, c_loc):
            c_scr[2 * h + d] = ao * c + al * cl
        n_new = [ao * n + al * jnp.sum(a, axis=0, keepdims=True)
                 for ao, n, al, a in zip(a_old, n0, a_loc, kw)]
        dlog = [jnp.where(incl[d], bc - br + x, -jnp.inf) for (h, d), bc, br, x in zip(chains, b_c, b_r, li_r)]
        inter = [bc + m for bc, m in zip(b_c, m0)]
        m_t = [jnp.maximum(x, jnp.max(dl, axis=1, keepdims=True)) for x, dl in zip(inter, dlog)]
        s = [_mm_nt(qq, kx) for qq, kx in zip(q, k)]
        p = [jnp.exp(dl - mt) * ss for dl, mt, ss in zip(dlog, m_t, s)]
        a_int = [jnp.exp(x - mt) for x, mt in zip(inter, m_t)]
        num = [ai * _mm(qq, c) + _mm(pp, vv) for ai, qq, c, pp, vv in zip(a_int, q, c0, p, v)]
        den = [ai * jnp.sum(qq * n, axis=1, keepdims=True) + jnp.sum(pp, axis=1, keepdims=True)
               for ai, qq, n, pp in zip(a_int, q, n0, p)]
        for (h, d), nu, de, mt in zip(chains, num, den, m_t):
            y_ref[0, pl.ds(r0[d], L), lanes(h)] += nu / jnp.maximum(jnp.abs(de), jnp.exp(-mt))
        return tuple(n_new), tuple(m_new)

    zn = tuple(jnp.zeros((1, LANES), F32) for _ in chains)
    zm = tuple(jnp.zeros((1, 1), F32) for _ in chains)
    lax.fori_loop(0, n_all, body, (zn, zm))


def _mlstm(qkvo, gates_t, *, n_ctx, n_all):
    bsz, s, _ = qkvo.shape
    h = MLSTM_HEADS
    hp = SCAN_HEADS_PER_STEP
    ng = h // hp
    wd = hp * LANES
    kern = functools.partial(_mlstm_kernel, n_ctx=n_ctx, n_all=n_all, hp=hp)
    blk = lambda off: pl.BlockSpec((1, s, wd), lambda i, j: (i, 0, j + off))
    return pl.pallas_call(
        kern, out_shape=jax.ShapeDtypeStruct((bsz, s, h * LANES), F32), grid=(bsz, ng),
        in_specs=[blk(0), blk(ng), blk(2 * ng),
                  pl.BlockSpec((1, hp * GATE_ROWS, s), lambda i, j: (i, j, 0))],
        out_specs=pl.BlockSpec((1, s, wd), lambda i, j: (i, 0, j)),
        scratch_shapes=[pltpu.VMEM((2 * hp, LANES, LANES), F32)],
        compiler_params=_cparams(("arbitrary", "arbitrary")), name="mlstm_scan",
    )(qkvo, qkvo, qkvo, gates_t)


def _gdn_kernel(q_ref, k_ref, v_ref, wq_ref, wk_ref, wv_ref, ab_ref, alog_ref, dtb_ref, y_ref,
                sh_scr, q_scr, k_scr, v_scr, s_scr, *, n_ctx, n_all, hp):
    L = CHUNK
    row, col = _chunk_masks(L)
    incl = (col <= row, col >= row)
    strict = (col < row, col > row)
    csum_r = (incl[1].astype(BF16), incl[0].astype(BF16))
    eye = row == col
    eye_f = jnp.where(eye, 1.0, 0.0)
    blk = tuple(jnp.right_shift(row, sh) == jnp.right_shift(col, sh) for sh in (4, 5, 6, 7))
    qscale = float(LANES) ** -0.5
    half = SHORT_K // 2
    edge = 16
    win_rows = 2 * L
    offsets = [o for o in range(-half, half + 1) if o != 0]
    wt = lax.broadcasted_iota(jnp.int32, (L, win_rows), 0)
    wu = lax.broadcasted_iota(jnp.int32, (L, win_rows), 1)
    for idx, o in enumerate(offsets):
        sh_scr[idx] = jnp.where(wu == wt + (edge + o), 1.0, 0.0).astype(BF16)
    gh = 2 if hp % 2 == 0 else 1
    gwid = gh * LANES
    zero_edge = jnp.zeros((edge, gwid), BF16)
    zero_fill = jnp.zeros((win_rows - L - 2 * edge, gwid), BF16)
    for src, w_ref, dst, norm in ((q_ref, wq_ref, q_scr, True), (k_ref, wk_ref, k_scr, True),
                                  (v_ref, wv_ref, v_scr, False)):
        for c in range(n_all):
            has_prev = c not in (0, n_ctx)
            has_next = c not in (n_ctx - 1, n_all - 1)
            for g0 in range(0, hp * LANES, gwid):
                ls = slice(g0, g0 + gwid)
                main = src[0, c * L:(c + 1) * L, ls]
                prev = src[0, c * L - edge:c * L, ls] if has_prev else zero_edge
                nxt = src[0, (c + 1) * L:(c + 1) * L + edge, ls] if has_next else zero_edge
                win = jnp.concatenate([prev, main, nxt, zero_fill], axis=0)
                acc = w_ref[half:half + 1, ls] * main.astype(F32)
                for idx, o in enumerate(offsets):
                    acc = acc + w_ref[o + half:o + half + 1, ls] * jnp.dot(
                        sh_scr[idx], win, preferred_element_type=F32)
                a = _silu(acc)
                for hh in range(gh):
                    ah = a[:, hh * LANES:(hh + 1) * LANES]
                    if norm:
                        ah = ah * lax.rsqrt(jnp.sum(ah * ah, axis=-1, keepdims=True) + EPS)
                    dst[c * L:(c + 1) * L, g0 + hh * LANES:g0 + (hh + 1) * LANES] = ah

    y_ref[...] = jnp.zeros(y_ref.shape, F32)
    s_scr[...] = jnp.zeros(s_scr.shape, F32)
    arate = jnp.exp(alog_ref[...])
    dtb = dtb_ref[...]

    chains = [(h, d) for h in range(hp) for d in (0, 1)]
    off_masks = [jnp.logical_and(blk[lvl], jnp.logical_not(blk[lvl - 1])) for lvl in range(1, len(blk))]

    def body(i, carry):
        r0 = (pl.multiple_of(i * L, L), pl.multiple_of(_bwd_chunk(i, n_ctx, n_all) * L, L))
        ab = [ab_ref[0, :, pl.ds(r0[d], L)] for d in (0, 1)]
        g8 = [-arate * _softplus(a + dtb) for a in ab]
        beta8 = [_sigmoid(a) for a in ab]
        gcs = [_mm_x01(g8[d], csum_r[d]) for d in (0, 1)]
        row = lambda arr, h, d, kind: arr[d][h * GATE_ROWS + 2 * d + kind:h * GATE_ROWS + 2 * d + kind + 1, :]
        gc_r = [row(gcs, h, d, 0) for h, d in chains]
        lanes = lambda h: slice(h * LANES, (h + 1) * LANES)
        q = [q_scr[pl.ds(r0[d], L), lanes(h)] * qscale for h, d in chains]
        k = [k_scr[pl.ds(r0[d], L), lanes(h)] for h, d in chains]
        v = [v_scr[pl.ds(r0[d], L), lanes(h)] for h, d in chains]
        cols_t = [_rows_to_columns(gcs[d], beta8[d]) for d in (0, 1)]
        nrow = hp * GATE_ROWS
        gc_c = [_column(cols_t[d], h * GATE_ROWS + 2 * d) for h, d in chains]
        beta_c = [_column(cols_t[d], nrow + h * GATE_ROWS + 2 * d + 1) for h, d in chains]
        total = [g[:, L - 1:L] if d == 0 else g[:, 0:1] for (h, d), g in zip(chains, gc_r)]
        dec = [jnp.exp(jnp.where(incl[d], c - r, -jnp.inf)) for (h, d), c, r in zip(chains, gc_c, gc_r)]
        kk = [_mm_nt(x, x) for x in k]
        a_mat = [jnp.where(strict[d], b * m * e, 0.0) for (h, d), b, m, e in zip(chains, beta_c, kk, dec)]
        egc = [jnp.exp(c) for c in gc_c]
        rhs = [jnp.concatenate([b * vv, (b * e) * kx], axis=1)
               for b, vv, e, kx in zip(beta_c, v, egc, k)]
        pw = [-jnp.where(blk[0], a, 0.0) for a in a_mat]
        t_inv = [eye_f + p for p in pw]
        for _ in range(3):
            pw = [_mm(p, p) for p in pw]
            t_inv = [t + _mm(t, p) for t, p in zip(t_inv, pw)]
        for om in off_masks:
            ta = [_mm(t, jnp.where(om, a, 0.0)) for t, a in zip(t_inv, a_mat)]
            t_inv = [t - _mm(x, t) for t, x in zip(t_inv, ta)]
        sol = [_mm(t, x) for t, x in zip(t_inv, rhs)]
        s0 = [s_scr[2 * h + d] for h, d in chains]
        v_new = [x[:, :LANES] - _mm(x[:, LANES:], s) for x, s in zip(sol, s0)]
        kdec = [jnp.exp(t - c) * kx for t, c, kx in zip(total, gc_c, k)]
        upd = [_mm_tn(kd, vn) for kd, vn in zip(kdec, v_new)]
        for (h, d), t, s, u in zip(chains, total, s0, upd):
            s_scr[2 * h + d] = jnp.exp(t) * s + u
        qk = [_mm_nt(qq, kx) * e for qq, kx, e in zip(q, k, dec)]
        o = [_mm(e * qq, s) + _mm(m, vn) for e, qq, s, m, vn in zip(egc, q, s0, qk, v_new)]
        for (h, d), oo in zip(chains, o):
            y_ref[0, pl.ds(r0[d], L), lanes(h)] += oo
        return carry

    lax.fori_loop(0, n_all, body, 0)


def _gdn(qkv, short_w, ab_t, alog_rows, dtb_rows, *, n_ctx, n_all):
    bsz, s, _ = qkv.shape
    h = GDN_HEADS
    hp = SCAN_HEADS_PER_STEP
    ng = h // hp
    wd = hp * LANES
    kern = functools.partial(_gdn_kernel, n_ctx=n_ctx, n_all=n_all, hp=hp)
    blk = lambda off: pl.BlockSpec((1, s, wd), lambda i, j: (i, 0, j + off))
    wblk = lambda off: pl.BlockSpec((SHORT_K, wd), lambda i, j: (0, j + off))
    return pl.pallas_call(
        kern, out_shape=jax.ShapeDtypeStruct((bsz, s, h * LANES), F32), grid=(bsz, ng),
        in_specs=[blk(0), blk(ng), blk(2 * ng), wblk(0), wblk(ng), wblk(2 * ng),
                  pl.BlockSpec((1, hp * GATE_ROWS, s), lambda i, j: (i, j, 0)),
                  pl.BlockSpec((hp * GATE_ROWS, 1), lambda i, j: (j, 0)),
                  pl.BlockSpec((hp * GATE_ROWS, 1), lambda i, j: (j, 0))],
        out_specs=pl.BlockSpec((1, s, wd), lambda i, j: (i, 0, j)),
        scratch_shapes=[pltpu.VMEM((SHORT_K - 1, CHUNK, 2 * CHUNK), BF16), pltpu.VMEM((s, wd), F32),
                        pltpu.VMEM((s, wd), F32), pltpu.VMEM((s, wd), F32),
                        pltpu.VMEM((2 * hp, LANES, LANES), F32)],
        compiler_params=_cparams(("arbitrary", "arbitrary")), name="gdn_scan",
    )(qkv, qkv, qkv, short_w, short_w, short_w, ab_t, alog_rows, dtb_rows)


def _conformer_kernel(glu_ref, w_ref, b_ref, lg_ref, lb_ref, o_ref, u_scr, *, sc, s_len):
    ch = o_ref.shape[-1]
    pad = 16
    rt = 32
    half = CONF_K // 2
    zeros_pad = jnp.zeros((pad, ch), F32)
    u_scr[0:pad, :] = zeros_pad
    u_scr[pad + sc:2 * pad + sc, :] = zeros_pad
    u_scr[2 * pad + s_len:3 * pad + s_len, :] = zeros_pad

    def fill(i, carry):
        r0 = pl.multiple_of(i * rt, rt)
        off = jnp.where(r0 < sc, pad, 2 * pad)
        a = glu_ref[0, pl.ds(r0, rt), 0:ch].astype(F32)
        gt = glu_ref[0, pl.ds(r0, rt), ch:2 * ch].astype(F32)
        u_scr[pl.ds(pl.multiple_of(r0 + off, 8), rt), :] = a * _sigmoid(gt)
        return carry

    lax.fori_loop(0, s_len // rt, fill, 0)

    def conv(i, carry):
        r0 = pl.multiple_of(i * rt, rt)
        off = jnp.where(r0 < sc, pad, 2 * pad)
        nwin = rt + 2 * pad
        start = pl.multiple_of(r0 + off - pad, 8)
        parts = []
        for l0 in range(0, ch, 2 * LANES):
            ls = slice(l0, l0 + 2 * LANES)
            win = u_scr[pl.ds(start, nwin), ls]
            acc = jnp.zeros((rt, 2 * LANES), F32)
            for res in range(8):
                shifted = win if res == 0 else pltpu.roll(win, nwin - res, axis=0)
                for a8 in range(0, nwin - rt + 1, 8):
                    j = a8 + res - (pad - half)
                    if 0 <= j < CONF_K:
                        acc = acc + w_ref[j:j + 1, ls] * shifted[a8:a8 + rt, :]
            parts.append(acc)
        y = jnp.concatenate(parts, axis=1) + b_ref[...]
        mu = jnp.mean(y, axis=-1, keepdims=True)
        yc = y - mu
        var = jnp.mean(yc * yc, axis=-1, keepdims=True)
        z = yc * lax.rsqrt(var + EPS) * lg_ref[...] + lb_ref[...]
        o_ref[0, pl.ds(r0, rt), :] = _silu(z).astype(o_ref.dtype)
        return carry

    lax.fori_loop(0, s_len // rt, conv, 0, unroll=4)


def _conformer(glu, dw, dw_b, ln_g, ln_b, *, sc):
    bsz, s, c2 = glu.shape
    ch = c2 // 2
    kern = functools.partial(_conformer_kernel, sc=sc, s_len=s)
    vec = lambda: pl.BlockSpec((1, ch), lambda i: (0, 0))
    return pl.pallas_call(
        kern, out_shape=jax.ShapeDtypeStruct((bsz, s, ch), BF16), grid=(bsz,),
        in_specs=[pl.BlockSpec((1, s, c2), lambda i: (i, 0, 0)),
                  pl.BlockSpec((CONF_K, ch), lambda i: (0, 0)), vec(), vec(), vec()],
        out_specs=pl.BlockSpec((1, s, ch), lambda i: (i, 0, 0)),
        scratch_shapes=[pltpu.VMEM((s + 48, ch), F32)],
        compiler_params=_cparams(("arbitrary",)), name="conformer_conv",
    )(glu, dw, dw_b.reshape(1, ch), ln_g.reshape(1, ch), ln_b.reshape(1, ch))


def _finish_kernel(y_ref, gate_ref, *rest, n_heads, even, mod_idx):
    if even:
        c_ref, x_ref, mod_ref, hg_ref, w_ref, o_ref = rest
    else:
        x_ref, mod_ref, hg_ref, w_ref, o_ref = rest
    d = x_ref.shape[-1]
    y = y_ref[0]
    parts = []
    for h in range(n_heads):
        yh = y[:, h * LANES:(h + 1) * LANES]
        ms = jnp.mean(yh * yh, axis=-1, keepdims=True)
        parts.append(yh * lax.rsqrt(ms + EPS))
    yn = jnp.concatenate(parts, axis=1)
    gate = gate_ref[0].astype(F32)
    if even:
        m = yn * hg_ref[...] * _sigmoid(gate)
        cat = jnp.concatenate([m.astype(BF16), c_ref[0].astype(BF16)], axis=1)
    else:
        cat = ((yn * hg_ref[...]) * _silu(gate)).astype(BF16)
    out = jnp.dot(cat, w_ref[...], preferred_element_type=F32)
    g1 = mod_ref[0, 0][:, mod_idx * d:(mod_idx + 1) * d]
    o_ref[0] = x_ref[0] + g1 * out


def _finish(y, gate_src, gate_blk, c_out, xs, mod, head_g, w_out, *, n_heads, tile0=0):
    bsz, s, d = xs.shape
    tm = TOKEN_TILE
    wdt = n_heads * LANES
    even = c_out is not None
    tok = lambda width, blk=0: pl.BlockSpec((1, tm, width), lambda i, j: (i, j + tile0, blk))
    in_specs = [tok(wdt), tok(wdt, gate_blk)]
    args = [y, gate_src]
    if even:
        in_specs.append(tok(c_out.shape[-1]))
        args.append(c_out)
    in_specs += [tok(d),
                 pl.BlockSpec((1, 1, 1, mod.shape[-1]), lambda i, j: (i, jnp.minimum(j + tile0, 1), 0, 0)),
                 pl.BlockSpec((1, wdt), lambda i, j: (0, 0)),
                 pl.BlockSpec(w_out.shape, lambda i, j: (0, 0))]
    args += [xs, mod, head_g.reshape(1, wdt), w_out]
    kern = functools.partial(_finish_kernel, n_heads=n_heads, even=even, mod_idx=2)
    return pl.pallas_call(
        kern, out_shape=jax.ShapeDtypeStruct((bsz, s, d), F32), grid=(bsz, s // tm - tile0),
        in_specs=in_specs, out_specs=tok(d),
        compiler_params=_cparams(("arbitrary", "arbitrary")), name="mixer_out",
    )(*args)


def _ffn_down_kernel(gm_ref, gp_ref, gn_ref, val_ref, x_ref, mod_ref, dw_ref, dwb_ref, wd_ref, *rest,
                     tile0, n_tiles_all, final, mod_idx):
    if final:
        fg_ref, o_ref, ext_scr, left_scr, right_scr, act_scr, sh_small, sh_big = rest
    else:
        o_ref, ext_scr, left_scr, right_scr, act_scr, sh_small, sh_big = rest
    tm = TOKEN_TILE
    gw = GRID_W
    f = gm_ref.shape[-1]
    d = x_ref.shape[-1]
    j = pl.program_id(1) + tile0
    lat = jnp.where(j > 0, 1.0, 0.0)
    jrow = jnp.broadcast_to(j, (gw, 1))
    has_prev = jrow >= 2
    has_next = jnp.logical_and(jrow >= 1, jrow <= n_tiles_all - 2)
    shifts = {}
    for n in (gw, tm):
        t = lax.broadcasted_iota(jnp.int32, (n, n), 0)
        u = lax.broadcasted_iota(jnp.int32, (n, n), 1)
        col = jnp.bitwise_and(t, gw - 1)
        ctx_seq = jnp.broadcast_to(j, (n, n)) == 0
        take_left = jnp.logical_and(u == t - 1, jnp.logical_or(col != 0, ctx_seq))
        take_right = jnp.logical_and(u == t + 1, jnp.logical_or(col != gw - 1, ctx_seq))
        shifts[n] = sh_small if n == gw else sh_big
        shifts[n][0] = jnp.where(take_left, 1.0, 0.0).astype(BF16)
        shifts[n][1] = jnp.where(take_right, 1.0, 0.0).astype(BF16)

    for n0 in range(0, f, FFN_CW):
        cols = slice(n0, n0 + FFN_CW)
        blocks = ((0, gw, jnp.where(has_prev, gp_ref[0, :, cols], 0.0).astype(BF16)),
                  (gw, tm, gm_ref[0, :, cols]),
                  (gw + tm, gw, jnp.where(has_next, gn_ref[0, :, cols], 0.0).astype(BF16)))
        slot = (n0 // FFN_CW) % 2
        for r0, n, g in blocks:
            ext_scr[slot, r0:r0 + n, :] = g.astype(F32)
            left_scr[slot, r0:r0 + n, :] = jnp.dot(shifts[n][0], g, preferred_element_type=F32)
            right_scr[slot, r0:r0 + n, :] = jnp.dot(shifts[n][1], g, preferred_element_type=F32)
        for rs in range(tm // gw):
            acc = jnp.zeros((gw, FFN_CW), F32)
            for dr in range(3):
                base = (rs + dr) * gw
                for dc, src in enumerate((left_scr, ext_scr, right_scr)):
                    wv = dw_ref[dr * 3 + dc:dr * 3 + dc + 1, cols]
                    if dr != 1:
                        wv = wv * lat
                    acc = acc + src[slot, base:base + gw, :] * wv
            gate = acc + dwb_ref[:, cols]
            a = _silu(gate) * val_ref[0, rs * gw:(rs + 1) * gw, cols]
            act_scr[rs * gw:(rs + 1) * gw, cols] = a.astype(BF16)

    out = jnp.dot(act_scr[...], wd_ref[...], preferred_element_type=F32)
    g2 = mod_ref[0, 0][:, mod_idx * d:(mod_idx + 1) * d]
    x2 = x_ref[0] + g2 * out
    if final:
        ms = jnp.mean(x2 * x2, axis=-1, keepdims=True)
        x2 = x2 * lax.rsqrt(ms + EPS) * fg_ref[...]
    o_ref[0] = x2


def _ffn_down(gate, val, xs, mod, dw9, dw_b, w_down, *, tile0=0, final_g=None):
    bsz, s, f = gate.shape
    d = xs.shape[-1]
    tm = TOKEN_TILE
    gw = GRID_W
    n_all = s // tm
    per = tm // gw
    n_rows = s // gw
    final = final_g is not None
    tok = lambda width: pl.BlockSpec((1, tm, width), lambda i, j: (i, j + tile0, 0))
    in_specs = [tok(f),
                pl.BlockSpec((1, gw, f), lambda i, j: (i, jnp.maximum((j + tile0) * per - 1, tile0 * per), 0)),
                pl.BlockSpec((1, gw, f), lambda i, j: (i, jnp.minimum((j + tile0 + 1) * per, n_rows - 1), 0)),
                tok(f), tok(d),
                pl.BlockSpec((1, 1, 1, mod.shape[-1]), lambda i, j: (i, jnp.minimum(j + tile0, 1), 0, 0)),
                pl.BlockSpec((9, f), lambda i, j: (0, 0)),
                pl.BlockSpec((1, f), lambda i, j: (0, 0)),
                pl.BlockSpec(w_down.shape, lambda i, j: (0, 0))]
    args = [gate, gate, gate, val, xs, mod, dw9, dw_b.reshape(1, f), w_down]
    if final:
        in_specs.append(pl.BlockSpec((1, d), lambda i, j: (0, 0)))
        args.append(final_g.reshape(1, d))
        out_shape = jax.ShapeDtypeStruct((bsz, s - tile0 * tm, d), F32)
        out_spec = pl.BlockSpec((1, tm, d), lambda i, j: (i, j, 0))
    else:
        out_shape = jax.ShapeDtypeStruct((bsz, s, d), F32)
        out_spec = tok(d)
    kern = functools.partial(_ffn_down_kernel, tile0=tile0, n_tiles_all=n_all, final=final, mod_idx=5)
    return pl.pallas_call(
        kern, out_shape=out_shape, grid=(bsz, n_all - tile0), in_specs=in_specs, out_specs=out_spec,
        scratch_shapes=[pltpu.VMEM((2, tm + 2 * gw, FFN_CW), F32)] * 3
        + [pltpu.VMEM((tm, f), BF16), pltpu.VMEM((2, gw, gw), BF16), pltpu.VMEM((2, tm, tm), BF16)],
        compiler_params=_cparams(("arbitrary", "arbitrary")), name="convglu_down",
    )(*args)


def _head_major_rows(w_cols, n_heads):
    d = w_cols.shape[0]
    w = w_cols.reshape(d, 4, n_heads).transpose(2, 1, 0)
    w = jnp.concatenate([w, jnp.zeros((n_heads, GATE_ROWS - 4, d), w.dtype)], axis=1)
    return w.reshape(n_heads * GATE_ROWS, d)


def _head_major_vec(b_cols, n_heads):
    b = b_cols.reshape(4, n_heads).T
    b = jnp.concatenate([b, jnp.zeros((n_heads, GATE_ROWS - 4), b.dtype)], axis=1)
    return b.reshape(n_heads * GATE_ROWS)


def kernel(x, c, ctx, c_ctx, ada_w, ada_b, norm1_g, norm2_g, e_w_in, e_b_in, e_head_g, e_conf_dw, e_conf_dw_b, e_conf_ln_g, e_conf_ln_b, e_w_out, o_w_in, o_short_w, o_a_log, o_dt_bias, o_head_g, o_w_out, f_w_up, f_dw, f_dw_b, f_w_down, final_g):
    bsz, t, d = x.shape
    sc = ctx.shape[1]
    depth = ada_w.shape[0]
    assert sc == TOKEN_TILE and t % TOKEN_TILE == 0 and t % CHUNK == 0 and sc % CHUNK == 0
    s = sc + t
    n_ctx, n_all = sc // CHUNK, s // CHUNK
    mw = MLSTM_HEADS * LANES
    gwd = GDN_HEADS * LANES
    ffn = f_w_down.shape[1]

    xs = jnp.concatenate([ctx, x], axis=1)

    rows = 16
    c_rows = jnp.concatenate([c, c_ctx[None, :], jnp.zeros((rows - bsz - 1, d), F32)], axis=0)
    mod_all = _ada_mod(c_rows, ada_w, ada_b)

    for layer in range(depth):
        last = layer == depth - 1
        tile0 = 1 if last else 0
        ml = mod_all[layer]
        mod = jnp.stack([jnp.broadcast_to(ml[bsz][None], (bsz, 6 * d)), ml[:bsz]], axis=1)[:, :, None, :]
        jj = layer // 2
        if layer % 2 == 0:
            w_in, b_in = e_w_in[jj], e_b_in[jj]
            g0 = 4 * mw
            g1 = g0 + 4 * MLSTM_HEADS
            w_main = jnp.concatenate([w_in[:, :g0], w_in[:, g1:]], axis=1).astype(BF16)
            b_main = jnp.concatenate([b_in[:g0], b_in[g1:]])
            w_rows = _head_major_rows(w_in[:, g0:g1], MLSTM_HEADS).astype(BF16)
            b_rows = _head_major_vec(b_in[g0:g1], MLSTM_HEADS)
            qkvo, glu, gates_t = _proj(xs, mod, norm1_g[layer], w_main, b_main, mod_idx=0,
                                       seg_widths=(g0, w_in.shape[1] - g1), w_rows=w_rows, b_rows=b_rows)
            y = _mlstm(qkvo, gates_t, n_ctx=n_ctx, n_all=n_all)
            c_out = _conformer(glu, e_conf_dw[jj], e_conf_dw_b[jj], e_conf_ln_g[jj], e_conf_ln_b[jj], sc=sc)
            x1 = _finish(y, qkvo, 3, c_out, xs, mod, e_head_g[jj], e_w_out[jj].astype(BF16),
                         n_heads=MLSTM_HEADS, tile0=tile0)
        else:
            w_in = o_w_in[jj]
            g0 = 4 * gwd
            w_main = w_in[:, :g0].astype(BF16)
            w_rows = _head_major_rows(w_in[:, g0:], GDN_HEADS).astype(BF16)
            b_rows = jnp.zeros((GDN_HEADS * GATE_ROWS,), F32)
            qkv, z, ab_t = _proj(xs, mod, norm1_g[layer], w_main, jnp.zeros((g0,), F32), mod_idx=0,
                                 seg_widths=(3 * gwd, gwd), w_rows=w_rows, b_rows=b_rows)
            zpad = jnp.zeros((GDN_HEADS, GATE_ROWS - 3), F32)
            alog_rows = jnp.concatenate([o_a_log[jj][0][:, None], jnp.zeros((GDN_HEADS, 1), F32),
                                         o_a_log[jj][1][:, None], zpad], axis=1).reshape(-1, 1)
            dtb_rows = jnp.concatenate([o_dt_bias[jj][0][:, None], jnp.zeros((GDN_HEADS, 1), F32),
                                        o_dt_bias[jj][1][:, None], zpad], axis=1).reshape(-1, 1)
            y = _gdn(qkv, o_short_w[jj], ab_t, alog_rows, dtb_rows, n_ctx=n_ctx, n_all=n_all)
            x1 = _finish(y, z, 0, None, xs, mod, jnp.tile(o_head_g[jj], GDN_HEADS), o_w_out[jj].astype(BF16),
                         n_heads=GDN_HEADS, tile0=tile0)
        gate, val = _proj(x1, mod, norm2_g[layer], f_w_up[layer].astype(BF16), jnp.zeros((2 * ffn,), F32),
                          mod_idx=3, seg_widths=(ffn, ffn), tile0=tile0)
        xs = _ffn_down(gate, val, x1, mod, f_dw[layer].reshape(9, ffn), f_dw_b[layer],
                       f_w_down[layer].astype(BF16), tile0=tile0, final_g=final_g if last else None)
    return xs
```

```python
import functools

import jax
import jax.numpy as jnp
from jax import lax
from jax.experimental import pallas as pl
from jax.experimental.pallas import tpu as pltpu

F32 = jnp.float32
BF16 = jnp.bfloat16
EPS = 1e-6

LANES = 128
CHUNK = LANES
TOKEN_TILE = 256
GRID_W = 64
MLSTM_HEADS = 4
GDN_HEADS = 8
CONF_K = 31
SHORT_K = 5
GATE_ROWS = 8
SCAN_HEADS_PER_STEP = 4
FFN_CW = 256
VMEM_LIMIT = 56 * 1024 * 1024


def _cparams(sem):
    return pltpu.CompilerParams(dimension_semantics=sem, vmem_limit_bytes=VMEM_LIMIT)


def _sigmoid(x):
    return 1.0 / (1.0 + jnp.exp(-x))


def _silu(x):
    return x * _sigmoid(x)


def _softplus(x):
    return jnp.maximum(x, 0.0) + jnp.log1p(jnp.exp(-jnp.abs(x)))


def _log_sigmoid(x):
    return jnp.minimum(x, 0.0) - jnp.log1p(jnp.exp(-jnp.abs(x)))


def _mm(a, b):
    return jnp.dot(a.astype(BF16), b.astype(BF16), preferred_element_type=F32)


def _mm_nt(a, b):
    return lax.dot_general(a.astype(BF16), b.astype(BF16), (((1,), (1,)), ((), ())),
                           preferred_element_type=F32)


def _mm_tn(a, b):
    return lax.dot_general(a.astype(BF16), b.astype(BF16), (((0,), (0,)), ((), ())),
                           preferred_element_type=F32)


def _split3(x):
    hi = x.astype(BF16)
    r = x - hi.astype(F32)
    mid = r.astype(BF16)
    lo = (r - mid.astype(F32)).astype(BF16)
    return hi, mid, lo


def _mm_x01(x, m01):
    hi, mid, lo = _split3(x)
    dot = functools.partial(jnp.dot, preferred_element_type=F32)
    return dot(hi, m01) + dot(mid, m01) + dot(lo, m01)


def _chunk_masks(L):
    row = lax.broadcasted_iota(jnp.int32, (L, L), 0)
    col = lax.broadcasted_iota(jnp.int32, (L, L), 1)
    return row, col


def _ada_kernel(c_ref, w_ref, b_ref, o_ref):
    c = c_ref[...]
    o_ref[0] = _mm(_silu(c), w_ref[0]) + b_ref[0]


def _ada_mod(c_rows, ada_w, ada_b):
    depth, d, n = ada_w.shape
    tn = 1536
    assert n % tn == 0
    rows = c_rows.shape[0]
    return pl.pallas_call(
        _ada_kernel,
        out_shape=jax.ShapeDtypeStruct((depth, rows, n), F32),
        grid=(depth, n // tn),
        in_specs=[pl.BlockSpec((rows, d), lambda l, j: (0, 0)),
                  pl.BlockSpec((1, d, tn), lambda l, j: (l, 0, j)),
                  pl.BlockSpec((1, 1, tn), lambda l, j: (l, 0, j))],
        out_specs=pl.BlockSpec((1, rows, tn), lambda l, j: (l, 0, j)),
        compiler_params=_cparams(("arbitrary", "arbitrary")),
        name="ada_mod",
    )(c_rows, ada_w, ada_b.reshape(depth, 1, n))


def _proj_kernel(x_ref, mod_ref, g_ref, w_ref, b_ref, *rest, mod_idx, segs, with_rows, tn):
    d = x_ref.shape[-1]
    if with_rows:
        wr_ref, br_ref = rest[:2]
        outs = rest[2:]
    else:
        outs = rest
    x = x_ref[0]
    ms = jnp.mean(x * x, axis=-1, keepdims=True)
    r = x * lax.rsqrt(ms + EPS)
    mod = mod_ref[0, 0]
    shift = mod[:, mod_idx * d:(mod_idx + 1) * d]
    scale = mod[:, (mod_idx + 1) * d:(mod_idx + 2) * d]
    h = r * g_ref[...] * (1.0 + scale) + shift
    hb = h.astype(BF16)
    for (off, width), o_ref in zip(segs, outs):
        for n0 in range(0, width, tn):
            nw = min(tn, width - n0)
            acc = jnp.dot(hb, w_ref[:, off + n0:off + n0 + nw], preferred_element_type=F32)
            o_ref[0, :, n0:n0 + nw] = (acc + b_ref[:, off + n0:off + n0 + nw]).astype(o_ref.dtype)
    if with_rows:
        rows = lax.dot_general(wr_ref[...], hb, (((1,), (1,)), ((), ())), preferred_element_type=F32)
        outs[-1][0] = rows + br_ref[...]


def _proj(xs, mod, g, w, b, *, mod_idx, seg_widths, w_rows=None, b_rows=None, tile0=0):
    bsz, s, d = xs.shape
    tm = TOKEN_TILE
    n_tiles = s // tm - tile0
    segs, off = [], 0
    for wd in seg_widths:
        segs.append((off, wd))
        off += wd
    assert off == w.shape[1]
    with_rows = w_rows is not None
    in_specs = [pl.BlockSpec((1, tm, d), lambda i, j: (i, j + tile0, 0)),
                pl.BlockSpec((1, 1, 1, mod.shape[-1]), lambda i, j: (i, jnp.minimum(j + tile0, 1), 0, 0)),
                pl.BlockSpec((1, d), lambda i, j: (0, 0)),
                pl.BlockSpec(w.shape, lambda i, j: (0, 0)),
                pl.BlockSpec((1, w.shape[1]), lambda i, j: (0, 0))]
    args = [xs, mod, g.reshape(1, d), w, b.reshape(1, -1)]
    out_shape = [jax.ShapeDtypeStruct((bsz, s, wd), BF16) for wd in seg_widths]
    out_specs = [pl.BlockSpec((1, tm, wd), lambda i, j: (i, j + tile0, 0)) for wd in seg_widths]
    if with_rows:
        nr = w_rows.shape[0]
        in_specs += [pl.BlockSpec(w_rows.shape, lambda i, j: (0, 0)),
                     pl.BlockSpec((nr, 1), lambda i, j: (0, 0))]
        args += [w_rows, b_rows.reshape(nr, 1)]
        out_shape.append(jax.ShapeDtypeStruct((bsz, nr, s), F32))
        out_specs.append(pl.BlockSpec((1, nr, tm), lambda i, j: (i, 0, j + tile0)))
    kern = functools.partial(_proj_kernel, mod_idx=mod_idx, segs=tuple(segs), with_rows=with_rows, tn=512)
    return pl.pallas_call(
        kern, out_shape=out_shape, grid=(bsz, n_tiles), in_specs=in_specs, out_specs=out_specs,
        compiler_params=_cparams(("arbitrary", "arbitrary")), name="norm_proj",
    )(*args)


def _bwd_chunk(i, n_ctx, n_all):
    return jnp.where(i < n_ctx, n_ctx - 1 - i, n_all - 1 - (i - n_ctx))


def _rows_to_columns(*row_blocks):
    n = row_blocks[0].shape[1]
    used = sum(b.shape[0] for b in row_blocks)
    stacked = jnp.concatenate(list(row_blocks) + [jnp.zeros((n - used, n), F32)], axis=0)
    return stacked.T


def _column(cols_t, r):
    return jnp.broadcast_to(cols_t[:, r:r + 1], (cols_t.shape[0], LANES))


def _mlstm_kernel(q_ref, k_ref, v_ref, g_ref, y_ref, c_scr, *, n_ctx, n_all, hp):
    L = CHUNK
    row, col = _chunk_masks(L)
    incl = (col <= row, col >= row)
    csum_r = (incl[1].astype(BF16), incl[0].astype(BF16))
    eye = row == col
    ones_b = jnp.ones((L, LANES), BF16)
    scale = float(LANES) ** -0.5

    y_ref[...] = jnp.zeros(y_ref.shape, F32)
    c_scr[...] = jnp.zeros(c_scr.shape, F32)

    chains = [(h, d) for h in range(hp) for d in (0, 1)]
    lanes = lambda h: slice(h * LANES, (h + 1) * LANES)

    def body(i, carry):
        n0, m0 = carry
        r0 = (pl.multiple_of(i * L, L), pl.multiple_of(_bwd_chunk(i, n_ctx, n_all) * L, L))
        gt = [g_ref[0, :, pl.ds(r0[d], L)] for d in (0, 1)]
        lsr = [_log_sigmoid(g) for g in gt]
        bsum = [_mm_x01(lsr[d], csum_r[d]) for d in (0, 1)]
        row = lambda arr, h, d, kind: arr[d][h * GATE_ROWS + 2 * d + kind:h * GATE_ROWS + 2 * d + kind + 1, :]
        li_r = [row(gt, h, d, 0) for h, d in chains]
        b_r = [row(bsum, h, d, 1) for h, d in chains]
        q = [q_ref[0, pl.ds(r0[d], L), lanes(h)].astype(F32) * scale for h, d in chains]
        k = [k_ref[0, pl.ds(r0[d], L), lanes(h)].astype(F32) for h, d in chains]
        v = [v_ref[0, pl.ds(r0[d], L), lanes(h)].astype(F32) for h, d in chains]
        lf_r = [row(lsr, h, d, 1) for h, d in chains]
        b_c = [_mm_x01(jnp.where(incl[d], f, 0.0), ones_b) for (h, d), f in zip(chains, lf_r)]
        li_c = [_mm_x01(jnp.where(eye, x, 0.0), ones_b) for x in li_r]
        total = [b[:, L - 1:L] if d == 0 else b[:, 0:1] for (h, d), b in zip(chains, b_r)]
        c0 = [c_scr[2 * h + d] for h, d in chains]
        m_loc = [jnp.max(t - b + x, axis=1, keepdims=True) for t, b, x in zip(total, b_r, li_r)]
        kw = [kx * jnp.exp(t - b + x - m) for kx, t, b, x, m in zip(k, total, b_c, li_c, m_loc)]
        c_loc = [_mm_tn(a, vv) for a, vv in zip(kw, v)]
        m_new = [jnp.maximum(t + m, ml) for t, m, ml in zip(total, m0, m_loc)]
        a_old = [jnp.exp(t + m - mn) for t, m, mn in zip(total, m0, m_new)]
        a_loc = [jnp.exp(ml - mn) for ml, mn in zip(m_loc, m_new)]
        for (h, d), ao, c, al, cl in zip(chains, a_old, c0, a_loc, c_loc):
            c_scr[2 * h + d] = ao * c + al * cl
        n_new = [ao * n + al * jnp.sum(a, axis=0, keepdims=True)
                 for ao, n, al, a in zip(a_old, n0, a_loc, kw)]
        dlog = [jnp.where(incl[d], bc - br + x, -jnp.inf) for (h, d), bc, br, x in zip(chains, b_c, b_r, li_r)]
        inter = [bc + m for bc, m in zip(b_c, m0)]
        m_t = [jnp.maximum(x, jnp.max(dl, axis=1, keepdims=True)) for x, dl in zip(inter, dlog)]
        s = [_mm_nt(qq, kx) for qq, kx in zip(q, k)]
        p = [jnp.exp(dl - mt) * ss for dl, mt, ss in zip(dlog, m_t, s)]
        a_int = [jnp.exp(x - mt) for x, mt in zip(inter, m_t)]
        num = [ai * _mm(qq, c) + _mm(pp, vv) for ai, qq, c, pp, vv in zip(a_int, q, c0, p, v)]
        den = [ai * jnp.sum(qq * n, axis=1, keepdims=True) + jnp.sum(pp, axis=1, keepdims=True)
               for ai, qq, n, pp in zip(a_int, q, n0, p)]
        for (h, d), nu, de, mt in zip(chains, num, den, m_t):
            y_ref[0, pl.ds(r0[d], L), lanes(h)] += nu / jnp.maximum(jnp.abs(de), jnp.exp(-mt))
        return tuple(n_new), tuple(m_new)

    zn = tuple(jnp.zeros((1, LANES), F32) for _ in chains)
    zm = tuple(jnp.zeros((1, 1), F32) for _ in chains)
    lax.fori_loop(0, n_all, body, (zn, zm))


def _mlstm(qkvo, gates_t, *, n_ctx, n_all):
    bsz, s, _ = qkvo.shape
    h = MLSTM_HEADS
    hp = SCAN_HEADS_PER_STEP
    ng = h // hp
    wd = hp * LANES
    kern = functools.partial(_mlstm_kernel, n_ctx=n_ctx, n_all=n_all, hp=hp)
    blk = lambda off: pl.BlockSpec((1, s, wd), lambda i, j: (i, 0, j + off))
    return pl.pallas_call(
        kern, out_shape=jax.ShapeDtypeStruct((bsz, s, h * LANES), F32), grid=(bsz, ng),
        in_specs=[blk(0), blk(ng), blk(2 * ng),
                  pl.BlockSpec((1, hp * GATE_ROWS, s), lambda i, j: (i, j, 0))],
        out_specs=pl.BlockSpec((1, s, wd), lambda i, j: (i, 0, j)),
        scratch_shapes=[pltpu.VMEM((2 * hp, LANES, LANES), F32)],
        compiler_params=_cparams(("arbitrary", "arbitrary")), name="mlstm_scan",
    )(qkvo, qkvo, qkvo, gates_t)


def _gdn_kernel(q_ref, k_ref, v_ref, wq_ref, wk_ref, wv_ref, ab_ref, alog_ref, dtb_ref, y_ref,
                sh_scr, q_scr, k_scr, v_scr, s_scr, *, n_ctx, n_all, hp):
    L = CHUNK
    row, col = _chunk_masks(L)
    incl = (col <= row, col >= row)
    strict = (col < row, col > row)
    csum_r = (incl[1].astype(BF16), incl[0].astype(BF16))
    eye = row == col
    eye_f = jnp.where(eye, 1.0, 0.0)
    blk = tuple(jnp.right_shift(row, sh) == jnp.right_shift(col, sh) for sh in (4, 5, 6, 7))
    qscale = float(LANES) ** -0.5
    half = SHORT_K // 2
    edge = 16
    win_rows = 2 * L
    offsets = [o for o in range(-half, half + 1) if o != 0]
    wt = lax.broadcasted_iota(jnp.int32, (L, win_rows), 0)
    wu = lax.broadcasted_iota(jnp.int32, (L, win_rows), 1)
    for idx, o in enumerate(offsets):
        sh_scr[idx] = jnp.where(wu == wt + (edge + o), 1.0, 0.0).astype(BF16)
    gh = 2 if hp % 2 == 0 else 1
    gwid = gh * LANES
    zero_edge = jnp.zeros((edge, gwid), BF16)
    zero_fill = jnp.zeros((win_rows - L - 2 * edge, gwid), BF16)
    for src, w_ref, dst, norm in ((q_ref, wq_ref, q_scr, True), (k_ref, wk_ref, k_scr, True),
                                  (v_ref, wv_ref, v_scr, False)):
        for c in range(n_all):
            has_prev = c not in (0, n_ctx)
            has_next = c not in (n_ctx - 1, n_all - 1)
            for g0 in range(0, hp * LANES, gwid):
                ls = slice(g0, g0 + gwid)
                main = src[0, c * L:(c + 1) * L, ls]
                prev = src[0, c * L - edge:c * L, ls] if has_prev else zero_edge
                nxt = src[0, (c + 1) * L:(c + 1) * L + edge, ls] if has_next else zero_edge
                win = jnp.concatenate([prev, main, nxt, zero_fill], axis=0)
                acc = w_ref[half:half + 1, ls] * main.astype(F32)
                for idx, o in enumerate(offsets):
                    acc = acc + w_ref[o + half:o + half + 1, ls] * jnp.dot(
                        sh_scr[idx], win, preferred_element_type=F32)
                a = _silu(acc)
                for hh in range(gh):
                    ah = a[:, hh * LANES:(hh + 1) * LANES]
                    if norm:
                        ah = ah * lax.rsqrt(jnp.sum(ah * ah, axis=-1, keepdims=True) + EPS)
                    dst[c * L:(c + 1) * L, g0 + hh * LANES:g0 + (hh + 1) * LANES] = ah

    y_ref[...] = jnp.zeros(y_ref.shape, F32)
    s_scr[...] = jnp.zeros(s_scr.shape, F32)
    arate = jnp.exp(alog_ref[...])
    dtb = dtb_ref[...]

    chains = [(h, d) for h in range(hp) for d in (0, 1)]
    off_masks = [jnp.logical_and(blk[lvl], jnp.logical_not(blk[lvl - 1])) for lvl in range(1, len(blk))]

    def body(i, carry):
        r0 = (pl.multiple_of(i * L, L), pl.multiple_of(_bwd_chunk(i, n_ctx, n_all) * L, L))
        ab = [ab_ref[0, :, pl.ds(r0[d], L)] for d in (0, 1)]
        g8 = [-arate * _softplus(a + dtb) for a in ab]
        beta8 = [_sigmoid(a) for a in ab]
        gcs = [_mm_x01(g8[d], csum_r[d]) for d in (0, 1)]
        row = lambda arr, h, d, kind: arr[d][h * GATE_ROWS + 2 * d + kind:h * GATE_ROWS + 2 * d + kind + 1, :]
        gc_r = [row(gcs, h, d, 0) for h, d in chains]
        lanes = lambda h: slice(h * LANES, (h + 1) * LANES)
        q = [q_scr[pl.ds(r0[d], L), lanes(h)] * qscale for h, d in chains]
        k = [k_scr[pl.ds(r0[d], L), lanes(h)] for h, d in chains]
        v = [v_scr[pl.ds(r0[d], L), lanes(h)] for h, d in chains]
        cols_t = [_rows_to_columns(gcs[d], beta8[d]) for d in (0, 1)]
        nrow = hp * GATE_ROWS
        gc_c = [_column(cols_t[d], h * GATE_ROWS + 2 * d) for h, d in chains]
        beta_c = [_column(cols_t[d], nrow + h * GATE_ROWS + 2 * d + 1) for h, d in chains]
        total = [g[:, L - 1:L] if d == 0 else g[:, 0:1] for (h, d), g in zip(chains, gc_r)]
        dec = [jnp.exp(jnp.where(incl[d], c - r, -jnp.inf)) for (h, d), c, r in zip(chains, gc_c, gc_r)]
        kk = [_mm_nt(x, x) for x in k]
        a_mat = [jnp.where(strict[d], b * m * e, 0.0) for (h, d), b, m, e in zip(chains, beta_c, kk, dec)]
        egc = [jnp.exp(c) for c in gc_c]
        rhs = [jnp.concatenate([b * vv, (b * e) * kx], axis=1)
               for b, vv, e, kx in zip(beta_c, v, egc, k)]
        pw = [-jnp.where(blk[0], a, 0.0) for a in a_mat]
        t_inv = [eye_f + p for p in pw]
        for _ in range(3):
            pw = [_mm(p, p) for p in pw]
            t_inv = [t + _mm(t, p) for t, p in zip(t_inv, pw)]
        for om in off_masks:
            ta = [_mm(t, jnp.where(om, a, 0.0)) for t, a in zip(t_inv, a_mat)]
            t_inv = [t - _mm(x, t) for t, x in zip(t_inv, ta)]
        sol = [_mm(t, x) for t, x in zip(t_inv, rhs)]
        s0 = [s_scr[2 * h + d] for h, d in chains]
        v_new = [x[:, :LANES] - _mm(x[:, LANES:], s) for x, s in zip(sol, s0)]
        kdec = [jnp.exp(t - c) * kx for t, c, kx in zip(total, gc_c, k)]
        upd = [_mm_tn(kd, vn) for kd, vn in zip(kdec, v_new)]
        for (h, d), t, s, u in zip(chains, total, s0, upd):
            s_scr[2 * h + d] = jnp.exp(t) * s + u
        qk = [_mm_nt(qq, kx) * e for qq, kx, e in zip(q, k, dec)]
        o = [_mm(e * qq, s) + _mm(m, vn) for e, qq, s, m, vn in zip(egc, q, s0, qk, v_new)]
        for (h, d), oo in zip(chains, o):
            y_ref[0, pl.ds(r0[d], L), lanes(h)] += oo
        return carry

    lax.fori_loop(0, n_all, body, 0)


def _gdn(qkv, short_w, ab_t, alog_rows, dtb_rows, *, n_ctx, n_all):
    bsz, s, _ = qkv.shape
    h = GDN_HEADS
    hp = SCAN_HEADS_PER_STEP
    ng = h // hp
    wd = hp * LANES
    kern = functools.partial(_gdn_kernel, n_ctx=n_ctx, n_all=n_all, hp=hp)
    blk = lambda off: pl.BlockSpec((1, s, wd), lambda i, j: (i, 0, j + off))
    wblk = lambda off: pl.BlockSpec((SHORT_K, wd), lambda i, j: (0, j + off))
    return pl.pallas_call(
        kern, out_shape=jax.ShapeDtypeStruct((bsz, s, h * LANES), F32), grid=(bsz, ng),
        in_specs=[blk(0), blk(ng), blk(2 * ng), wblk(0), wblk(ng), wblk(2 * ng),
                  pl.BlockSpec((1, hp * GATE_ROWS, s), lambda i, j: (i, j, 0)),
                  pl.BlockSpec((hp * GATE_ROWS, 1), lambda i, j: (j, 0)),
                  pl.BlockSpec((hp * GATE_ROWS, 1), lambda i, j: (j, 0))],
        out_specs=pl.BlockSpec((1, s, wd), lambda i, j: (i, 0, j)),
        scratch_shapes=[pltpu.VMEM((SHORT_K - 1, CHUNK, 2 * CHUNK), BF16), pltpu.VMEM((s, wd), F32),
                        pltpu.VMEM((s, wd), F32), pltpu.VMEM((s, wd), F32),
                        pltpu.VMEM((2 * hp, LANES, LANES), F32)],
        compiler_params=_cparams(("arbitrary", "arbitrary")), name="gdn_scan",
    )(qkv, qkv, qkv, short_w, short_w, short_w, ab_t, alog_rows, dtb_rows)


def _conformer_kernel(glu_ref, w_ref, b_ref, lg_ref, lb_ref, o_ref, u_scr, *, sc, s_len):
    ch = o_ref.shape[-1]
    pad = 16
    rt = 32
    half = CONF_K // 2
    zeros_pad = jnp.zeros((pad, ch), F32)
    u_scr[0:pad, :] = zeros_pad
    u_scr[pad + sc:2 * pad + sc, :] = zeros_pad
    u_scr[2 * pad + s_len:3 * pad + s_len, :] = zeros_pad

    def fill(i, carry):
        r0 = pl.multiple_of(i * rt, rt)
        off = jnp.where(r0 < sc, pad, 2 * pad)
        a = glu_ref[0, pl.ds(r0, rt), 0:ch].astype(F32)
        gt = glu_ref[0, pl.ds(r0, rt), ch:2 * ch].astype(F32)
        u_scr[pl.ds(pl.multiple_of(r0 + off, 8), rt), :] = a * _sigmoid(gt)
        return carry

    lax.fori_loop(0, s_len // rt, fill, 0)

    def conv(i, carry):
        r0 = pl.multiple_of(i * rt, rt)
        off = jnp.where(r0 < sc, pad, 2 * pad)
        nwin = rt + 2 * pad
        start = pl.multiple_of(r0 + off - pad, 8)
        parts = []
        for l0 in range(0, ch, 2 * LANES):
            ls = slice(l0, l0 + 2 * LANES)
            win = u_scr[pl.ds(start, nwin), ls]
            acc = jnp.zeros((rt, 2 * LANES), F32)
            for res in range(8):
                shifted = win if res == 0 else pltpu.roll(win, nwin - res, axis=0)
                for a8 in range(0, nwin - rt + 1, 8):
                    j = a8 + res - (pad - half)
                    if 0 <= j < CONF_K:
                        acc = acc + w_ref[j:j + 1, ls] * shifted[a8:a8 + rt, :]
            parts.append(acc)
        y = jnp.concatenate(parts, axis=1) + b_ref[...]
        mu = jnp.mean(y, axis=-1, keepdims=True)
        yc = y - mu
        var = jnp.mean(yc * yc, axis=-1, keepdims=True)
        z = yc * lax.rsqrt(var + EPS) * lg_ref[...] + lb_ref[...]
        o_ref[0, pl.ds(r0, rt), :] = _silu(z).astype(o_ref.dtype)
        return carry

    lax.fori_loop(0, s_len // rt, conv, 0, unroll=4)


def _conformer(glu, dw, dw_b, ln_g, ln_b, *, sc):
    bsz, s, c2 = glu.shape
    ch = c2 // 2
    kern = functools.partial(_conformer_kernel, sc=sc, s_len=s)
    vec = lambda: pl.BlockSpec((1, ch), lambda i: (0, 0))
    return pl.pallas_call(
        kern, out_shape=jax.ShapeDtypeStruct((bsz, s, ch), BF16), grid=(bsz,),
        in_specs=[pl.BlockSpec((1, s, c2), lambda i: (i, 0, 0)),
                  pl.BlockSpec((CONF_K, ch), lambda i: (0, 0)), vec(), vec(), vec()],
        out_specs=pl.BlockSpec((1, s, ch), lambda i: (i, 0, 0)),
        scratch_shapes=[pltpu.VMEM((s + 48, ch), F32)],
        compiler_params=_cparams(("arbitrary",)), name="conformer_conv",
    )(glu, dw, dw_b.reshape(1, ch), ln_g.reshape(1, ch), ln_b.reshape(1, ch))


def _finish_kernel(y_ref, gate_ref, *rest, n_heads, even, mod_idx, up_mod_idx, tn):
    if even:
        c_ref, x_ref, mod_ref, hg_ref, w_ref, ng_ref, wu_ref, o_ref, og_ref, ov_ref = rest
    else:
        x_ref, mod_ref, hg_ref, w_ref, ng_ref, wu_ref, o_ref, og_ref, ov_ref = rest
    d = x_ref.shape[-1]
    y = y_ref[0]
    parts = []
    for h in range(n_heads):
        yh = y[:, h * LANES:(h + 1) * LANES]
        ms = jnp.mean(yh * yh, axis=-1, keepdims=True)
        parts.append(yh * lax.rsqrt(ms + EPS))
    yn = jnp.concatenate(parts, axis=1)
    gate = gate_ref[0].astype(F32)
    if even:
        m = yn * hg_ref[...] * _sigmoid(gate)
        cat = jnp.concatenate([m.astype(BF16), c_ref[0].astype(BF16)], axis=1)
    else:
        cat = ((yn * hg_ref[...]) * _silu(gate)).astype(BF16)
    out = jnp.dot(cat, w_ref[...], preferred_element_type=F32)
    mod = mod_ref[0, 0]
    x1 = x_ref[0] + mod[:, mod_idx * d:(mod_idx + 1) * d] * out
    o_ref[0] = x1
    ms = jnp.mean(x1 * x1, axis=-1, keepdims=True)
    r = x1 * lax.rsqrt(ms + EPS)
    shift = mod[:, up_mod_idx * d:(up_mod_idx + 1) * d]
    scale = mod[:, (up_mod_idx + 1) * d:(up_mod_idx + 2) * d]
    hb = (r * ng_ref[...] * (1.0 + scale) + shift).astype(BF16)
    fdim = og_ref.shape[-1]
    for up_ref, off in ((og_ref, 0), (ov_ref, fdim)):
        for n0 in range(0, fdim, tn):
            nw = min(tn, fdim - n0)
            acc = jnp.dot(hb, wu_ref[:, off + n0:off + n0 + nw], preferred_element_type=F32)
            up_ref[0, :, n0:n0 + nw] = acc.astype(up_ref.dtype)


def _finish(y, gate_src, gate_blk, c_out, xs, mod, head_g, w_out, norm2_g, w_up, *, n_heads, tile0=0):
    bsz, s, d = xs.shape
    tm = TOKEN_TILE
    wdt = n_heads * LANES
    fdim = w_up.shape[1] // 2
    even = c_out is not None
    tok = lambda width, blk=0: pl.BlockSpec((1, tm, width), lambda i, j: (i, j + tile0, blk))
    in_specs = [tok(wdt), tok(wdt, gate_blk)]
    args = [y, gate_src]
    if even:
        in_specs.append(tok(c_out.shape[-1]))
        args.append(c_out)
    in_specs += [tok(d),
                 pl.BlockSpec((1, 1, 1, mod.shape[-1]), lambda i, j: (i, jnp.minimum(j + tile0, 1), 0, 0)),
                 pl.BlockSpec((1, wdt), lambda i, j: (0, 0)),
                 pl.BlockSpec(w_out.shape, lambda i, j: (0, 0)),
                 pl.BlockSpec((1, d), lambda i, j: (0, 0)),
                 pl.BlockSpec(w_up.shape, lambda i, j: (0, 0))]
    args += [xs, mod, head_g.reshape(1, wdt), w_out, norm2_g.reshape(1, d), w_up]
    kern = functools.partial(_finish_kernel, n_heads=n_heads, even=even, mod_idx=2, up_mod_idx=3, tn=512)
    return pl.pallas_call(
        kern, grid=(bsz, s // tm - tile0), in_specs=in_specs,
        out_shape=[jax.ShapeDtypeStruct((bsz, s, d), F32), jax.ShapeDtypeStruct((bsz, s, fdim), BF16),
                   jax.ShapeDtypeStruct((bsz, s, fdim), BF16)],
        out_specs=[tok(d), tok(fdim), tok(fdim)],
        compiler_params=_cparams(("arbitrary", "arbitrary")), name="mixer_out_ffn_up",
    )(*args)


def _ffn_down_kernel(gm_ref, gp_ref, gn_ref, val_ref, x_ref, mod_ref, dw_ref, dwb_ref, wd_ref, *rest,
                     tile0, n_tiles_all, final, mod_idx):
    if final:
        fg_ref, o_ref, ext_scr, left_scr, right_scr, act_scr, sh_small, sh_big = rest
    else:
        o_ref, ext_scr, left_scr, right_scr, act_scr, sh_small, sh_big = rest
    tm = TOKEN_TILE
    gw = GRID_W
    f = gm_ref.shape[-1]
    d = x_ref.shape[-1]
    j = pl.program_id(1) + tile0
    lat = jnp.where(j > 0, 1.0, 0.0)
    jrow = jnp.broadcast_to(j, (gw, 1))
    has_prev = jrow >= 2
    has_next = jnp.logical_and(jrow >= 1, jrow <= n_tiles_all - 2)
    shifts = {}
    for n in (gw, tm):
        t = lax.broadcasted_iota(jnp.int32, (n, n), 0)
        u = lax.broadcasted_iota(jnp.int32, (n, n), 1)
        col = jnp.bitwise_and(t, gw - 1)
        ctx_seq = jnp.broadcast_to(j, (n, n)) == 0
        take_left = jnp.logical_and(u == t - 1, jnp.logical_or(col != 0, ctx_seq))
        take_right = jnp.logical_and(u == t + 1, jnp.logical_or(col != gw - 1, ctx_seq))
        shifts[n] = sh_small if n == gw else sh_big
        shifts[n][0] = jnp.where(take_left, 1.0, 0.0).astype(BF16)
        shifts[n][1] = jnp.where(take_right, 1.0, 0.0).astype(BF16)

    for n0 in range(0, f, FFN_CW):
        cols = slice(n0, n0 + FFN_CW)
        blocks = ((0, gw, jnp.where(has_prev, gp_ref[0, :, cols], 0.0).astype(BF16)),
                  (gw, tm, gm_ref[0, :, cols]),
                  (gw + tm, gw, jnp.where(has_next, gn_ref[0, :, cols], 0.0).astype(BF16)))
        slot = (n0 // FFN_CW) % 2
        for r0, n, g in blocks:
            ext_scr[slot, r0:r0 + n, :] = g.astype(F32)
            left_scr[slot, r0:r0 + n, :] = jnp.dot(shifts[n][0], g, preferred_element_type=F32)
            right_scr[slot, r0:r0 + n, :] = jnp.dot(shifts[n][1], g, preferred_element_type=F32)
        for rs in range(tm // gw):
            acc = jnp.zeros((gw, FFN_CW), F32)
            for dr in range(3):
                base = (rs + dr) * gw
                for dc, src in enumerate((left_scr, ext_scr, right_scr)):
                    wv = dw_ref[dr * 3 + dc:dr * 3 + dc + 1, cols]
                    if dr != 1:
                        wv = wv * lat
                    acc = acc + src[slot, base:base + gw, :] * wv
            gate = acc + dwb_ref[:, cols]
            a = _silu(gate) * val_ref[0, rs * gw:(rs + 1) * gw, cols]
            act_scr[rs * gw:(rs + 1) * gw, cols] = a.astype(BF16)

    out = jnp.dot(act_scr[...], wd_ref[...], preferred_element_type=F32)
    g2 = mod_ref[0, 0][:, mod_idx * d:(mod_idx + 1) * d]
    x2 = x_ref[0] + g2 * out
    if final:
        ms = jnp.mean(x2 * x2, axis=-1, keepdims=True)
        x2 = x2 * lax.rsqrt(ms + EPS) * fg_ref[...]
    o_ref[0] = x2


def _ffn_down(gate, val, xs, mod, dw9, dw_b, w_down, *, tile0=0, final_g=None):
    bsz, s, f = gate.shape
    d = xs.shape[-1]
    tm = TOKEN_TILE
    gw = GRID_W
    n_all = s // tm
    per = tm // gw
    n_rows = s // gw
    final = final_g is not None
    tok = lambda width: pl.BlockSpec((1, tm, width), lambda i, j: (i, j + tile0, 0))
    in_specs = [tok(f),
                pl.BlockSpec((1, gw, f), lambda i, j: (i, jnp.maximum((j + tile0) * per - 1, tile0 * per), 0)),
                pl.BlockSpec((1, gw, f), lambda i, j: (i, jnp.minimum((j + tile0 + 1) * per, n_rows - 1), 0)),
                tok(f), tok(d),
                pl.BlockSpec((1, 1, 1, mod.shape[-1]), lambda i, j: (i, jnp.minimum(j + tile0, 1), 0, 0)),
                pl.BlockSpec((9, f), lambda i, j: (0, 0)),
                pl.BlockSpec((1, f), lambda i, j: (0, 0)),
                pl.BlockSpec(w_down.shape, lambda i, j: (0, 0))]
    args = [gate, gate, gate, val, xs, mod, dw9, dw_b.reshape(1, f), w_down]
    if final:
        in_specs.append(pl.BlockSpec((1, d), lambda i, j: (0, 0)))
        args.append(final_g.reshape(1, d))
        out_shape = jax.ShapeDtypeStruct((bsz, s - tile0 * tm, d), F32)
        out_spec = pl.BlockSpec((1, tm, d), lambda i, j: (i, j, 0))
    else:
        out_shape = jax.ShapeDtypeStruct((bsz, s, d), F32)
        out_spec = tok(d)
    kern = functools.partial(_ffn_down_kernel, tile0=tile0, n_tiles_all=n_all, final=final, mod_idx=5)
    return pl.pallas_call(
        kern, out_shape=out_shape, grid=(bsz, n_all - tile0), in_specs=in_specs, out_specs=out_spec,
        scratch_shapes=[pltpu.VMEM((2, tm + 2 * gw, FFN_CW), F32)] * 3
        + [pltpu.VMEM((tm, f), BF16), pltpu.VMEM((2, gw, gw), BF16), pltpu.VMEM((2, tm, tm), BF16)],
        compiler_params=_cparams(("arbitrary", "arbitrary")), name="convglu_down",
    )(*args)


def _head_major_rows(w_cols, n_heads):
    d = w_cols.shape[0]
    w = w_cols.reshape(d, 4, n_heads).transpose(2, 1, 0)
    w = jnp.concatenate([w, jnp.zeros((n_heads, GATE_ROWS - 4, d), w.dtype)], axis=1)
    return w.reshape(n_heads * GATE_ROWS, d)


def _head_major_vec(b_cols, n_heads):
    b = b_cols.reshape(4, n_heads).T
    b = jnp.concatenate([b, jnp.zeros((n_heads, GATE_ROWS - 4), b.dtype)], axis=1)
    return b.reshape(n_heads * GATE_ROWS)


def kernel(x, c, ctx, c_ctx, ada_w, ada_b, norm1_g, norm2_g, e_w_in, e_b_in, e_head_g, e_conf_dw, e_conf_dw_b, e_conf_ln_g, e_conf_ln_b, e_w_out, o_w_in, o_short_w, o_a_log, o_dt_bias, o_head_g, o_w_out, f_w_up, f_dw, f_dw_b, f_w_down, final_g):
    bsz, t, d = x.shape
    sc = ctx.shape[1]
    depth = ada_w.shape[0]
    assert sc == TOKEN_TILE and t % TOKEN_TILE == 0 and t % CHUNK == 0 and sc % CHUNK == 0
    s = sc + t
    n_ctx, n_all = sc // CHUNK, s // CHUNK
    mw = MLSTM_HEADS * LANES
    gwd = GDN_HEADS * LANES
    ffn = f_w_down.shape[1]

    xs = jnp.concatenate([ctx, x], axis=1)

    rows = 16
    c_rows = jnp.concatenate([c, c_ctx[None, :], jnp.zeros((rows - bsz - 1, d), F32)], axis=0)
    mod_all = _ada_mod(c_rows, ada_w, ada_b)

    for layer in range(depth):
        last = layer == depth - 1
        tile0 = 1 if last else 0
        ml = mod_all[layer]
        mod = jnp.stack([jnp.broadcast_to(ml[bsz][None], (bsz, 6 * d)), ml[:bsz]], axis=1)[:, :, None, :]
        jj = layer // 2
        w_up = f_w_up[layer].astype(BF16)
        if layer % 2 == 0:
            w_in, b_in = e_w_in[jj], e_b_in[jj]
            g0 = 4 * mw
            g1 = g0 + 4 * MLSTM_HEADS
            w_main = jnp.concatenate([w_in[:, :g0], w_in[:, g1:]], axis=1).astype(BF16)
            b_main = jnp.concatenate([b_in[:g0], b_in[g1:]])
            w_rows = _head_major_rows(w_in[:, g0:g1], MLSTM_HEADS).astype(BF16)
            b_rows = _head_major_vec(b_in[g0:g1], MLSTM_HEADS)
            qkvo, glu, gates_t = _proj(xs, mod, norm1_g[layer], w_main, b_main, mod_idx=0,
                                       seg_widths=(g0, w_in.shape[1] - g1), w_rows=w_rows, b_rows=b_rows)
            y = _mlstm(qkvo, gates_t, n_ctx=n_ctx, n_all=n_all)
            c_out = _conformer(glu, e_conf_dw[jj], e_conf_dw_b[jj], e_conf_ln_g[jj], e_conf_ln_b[jj], sc=sc)
            x1, gate, val = _finish(y, qkvo, 3, c_out, xs, mod, e_head_g[jj], e_w_out[jj].astype(BF16),
                                    norm2_g[layer], w_up, n_heads=MLSTM_HEADS, tile0=tile0)
        else:
            w_in = o_w_in[jj]
            g0 = 4 * gwd
            w_main = w_in[:, :g0].astype(BF16)
            w_rows = _head_major_rows(w_in[:, g0:], GDN_HEADS).astype(BF16)
            b_rows = jnp.zeros((GDN_HEADS * GATE_ROWS,), F32)
            qkv, z, ab_t = _proj(xs, mod, norm1_g[layer], w_main, jnp.zeros((g0,), F32), mod_idx=0,
                                 seg_widths=(3 * gwd, gwd), w_rows=w_rows, b_rows=b_rows)
            zpad = jnp.zeros((GDN_HEADS, GATE_ROWS - 3), F32)
            alog_rows = jnp.concatenate([o_a_log[jj][0][:, None], jnp.zeros((GDN_HEADS, 1), F32),
                                         o_a_log[jj][1][:, None], zpad], axis=1).reshape(-1, 1)
            dtb_rows = jnp.concatenate([o_dt_bias[jj][0][:, None], jnp.zeros((GDN_HEADS, 1), F32),
                                        o_dt_bias[jj][1][:, None], zpad], axis=1).reshape(-1, 1)
            y = _gdn(qkv, o_short_w[jj], ab_t, alog_rows, dtb_rows, n_ctx=n_ctx, n_all=n_all)
            x1, gate, val = _finish(y, z, 0, None, xs, mod, jnp.tile(o_head_g[jj], GDN_HEADS),
                                    o_w_out[jj].astype(BF16), norm2_g[layer], w_up, n_heads=GDN_HEADS, tile0=tile0)
        xs = _ffn_down(gate, val, x1, mod, f_dw[layer].reshape(9, ffn), f_dw_b[layer],
                       f_w_down[layer].astype(BF16), tile0=tile0, final_g=final_g if last else None)
    return xs
```

```python
import functools

import jax
import jax.numpy as jnp
from jax import lax
from jax.experimental import pallas as pl
from jax.experimental.pallas import tpu as pltpu

F32 = jnp.float32
BF16 = jnp.bfloat16
EPS = 1e-6

LANES = 128
CHUNK = LANES
TOKEN_TILE = 256
GRID_W = 64
MLSTM_HEADS = 4
GDN_HEADS = 8
CONF_K = 31
SHORT_K = 5
GATE_ROWS = 8
SCAN_HEADS_PER_STEP = 4
GDN_HEADS_PER_STEP = 8
FFN_CW = 256
VMEM_LIMIT = 56 * 1024 * 1024


def _cparams(sem):
    return pltpu.CompilerParams(dimension_semantics=sem, vmem_limit_bytes=VMEM_LIMIT)


def _sigmoid(x):
    return 1.0 / (1.0 + jnp.exp(-x))


def _silu(x):
    return x * _sigmoid(x)


def _softplus(x):
    return jnp.maximum(x, 0.0) + jnp.log1p(jnp.exp(-jnp.abs(x)))


def _log_sigmoid(x):
    return jnp.minimum(x, 0.0) - jnp.log1p(jnp.exp(-jnp.abs(x)))


def _mm(a, b):
    return jnp.dot(a.astype(BF16), b.astype(BF16), preferred_element_type=F32)


def _mm_nt(a, b):
    return lax.dot_general(a.astype(BF16), b.astype(BF16), (((1,), (1,)), ((), ())),
                           preferred_element_type=F32)


def _mm_tn(a, b):
    return lax.dot_general(a.astype(BF16), b.astype(BF16), (((0,), (0,)), ((), ())),
                           preferred_element_type=F32)


def _split3(x):
    hi = x.astype(BF16)
    r = x - hi.astype(F32)
    mid = r.astype(BF16)
    lo = (r - mid.astype(F32)).astype(BF16)
    return hi, mid, lo


def _mm_x01(x, m01):
    hi, mid, lo = _split3(x)
    dot = functools.partial(jnp.dot, preferred_element_type=F32)
    return dot(hi, m01) + dot(mid, m01) + dot(lo, m01)


def _chunk_masks(L):
    row = lax.broadcasted_iota(jnp.int32, (L, L), 0)
    col = lax.broadcasted_iota(jnp.int32, (L, L), 1)
    return row, col


def _ada_kernel(c_ref, w_ref, b_ref, o_ref):
    c = c_ref[...]
    o_ref[0] = _mm(_silu(c), w_ref[0]) + b_ref[0]


def _ada_mod(c_rows, ada_w, ada_b):
    depth, d, n = ada_w.shape
    tn = 1536
    assert n % tn == 0
    rows = c_rows.shape[0]
    return pl.pallas_call(
        _ada_kernel,
        out_shape=jax.ShapeDtypeStruct((depth, rows, n), F32),
        grid=(depth, n // tn),
        in_specs=[pl.BlockSpec((rows, d), lambda l, j: (0, 0)),
                  pl.BlockSpec((1, d, tn), lambda l, j: (l, 0, j)),
                  pl.BlockSpec((1, 1, tn), lambda l, j: (l, 0, j))],
        out_specs=pl.BlockSpec((1, rows, tn), lambda l, j: (l, 0, j)),
        compiler_params=_cparams(("arbitrary", "arbitrary")),
        name="ada_mod",
    )(c_rows, ada_w, ada_b.reshape(depth, 1, n))


def _proj_kernel(x_ref, mod_ref, g_ref, w_ref, b_ref, *rest, mod_idx, segs, with_rows, tn):
    d = x_ref.shape[-1]
    if with_rows:
        wr_ref, br_ref = rest[:2]
        outs = rest[2:]
    else:
        outs = rest
    x = x_ref[0]
    ms = jnp.mean(x * x, axis=-1, keepdims=True)
    r = x * lax.rsqrt(ms + EPS)
    mod = mod_ref[0, 0]
    shift = mod[:, mod_idx * d:(mod_idx + 1) * d]
    scale = mod[:, (mod_idx + 1) * d:(mod_idx + 2) * d]
    h = r * g_ref[...] * (1.0 + scale) + shift
    hb = h.astype(BF16)
    for (off, width), o_ref in zip(segs, outs):
        for n0 in range(0, width, tn):
            nw = min(tn, width - n0)
            acc = jnp.dot(hb, w_ref[:, off + n0:off + n0 + nw], preferred_element_type=F32)
            o_ref[0, :, n0:n0 + nw] = (acc + b_ref[:, off + n0:off + n0 + nw]).astype(o_ref.dtype)
    if with_rows:
        rows = lax.dot_general(wr_ref[...], hb, (((1,), (1,)), ((), ())), preferred_element_type=F32)
        outs[-1][0] = rows + br_ref[...]


def _proj(xs, mod, g, w, b, *, mod_idx, seg_widths, w_rows=None, b_rows=None, tile0=0):
    bsz, s, d = xs.shape
    tm = TOKEN_TILE
    n_tiles = s // tm - tile0
    segs, off = [], 0
    for wd in seg_widths:
        segs.append((off, wd))
        off += wd
    assert off == w.shape[1]
    with_rows = w_rows is not None
    in_specs = [pl.BlockSpec((1, tm, d), lambda i, j: (i, j + tile0, 0)),
                pl.BlockSpec((1, 1, 1, mod.shape[-1]), lambda i, j: (i, jnp.minimum(j + tile0, 1), 0, 0)),
                pl.BlockSpec((1, d), lambda i, j: (0, 0)),
                pl.BlockSpec(w.shape, lambda i, j: (0, 0)),
                pl.BlockSpec((1, w.shape[1]), lambda i, j: (0, 0))]
    args = [xs, mod, g.reshape(1, d), w, b.reshape(1, -1)]
    out_shape = [jax.ShapeDtypeStruct((bsz, s, wd), BF16) for wd in seg_widths]
    out_specs = [pl.BlockSpec((1, tm, wd), lambda i, j: (i, j + tile0, 0)) for wd in seg_widths]
    if with_rows:
        nr = w_rows.shape[0]
        in_specs += [pl.BlockSpec(w_rows.shape, lambda i, j: (0, 0)),
                     pl.BlockSpec((nr, 1), lambda i, j: (0, 0))]
        args += [w_rows, b_rows.reshape(nr, 1)]
        out_shape.append(jax.ShapeDtypeStruct((bsz, nr, s), F32))
        out_specs.append(pl.BlockSpec((1, nr, tm), lambda i, j: (i, 0, j + tile0)))
    kern = functools.partial(_proj_kernel, mod_idx=mod_idx, segs=tuple(segs), with_rows=with_rows, tn=512)
    return pl.pallas_call(
        kern, out_shape=out_shape, grid=(bsz, n_tiles), in_specs=in_specs, out_specs=out_specs,
        compiler_params=_cparams(("arbitrary", "arbitrary")), name="norm_proj",
    )(*args)


def _bwd_chunk(i, n_ctx, n_all):
    return jnp.where(i < n_ctx, n_ctx - 1 - i, n_all - 1 - (i - n_ctx))


def _rows_to_columns(*row_blocks):
    n = row_blocks[0].shape[1]
    used = sum(b.shape[0] for b in row_blocks)
    pad = [jnp.zeros((n - used, n), F32)] if used < n else []
    return jnp.concatenate(list(row_blocks) + pad, axis=0).T


def _column(cols_t, r):
    return jnp.broadcast_to(cols_t[:, r:r + 1], (cols_t.shape[0], LANES))


def _mlstm_kernel(q_ref, k_ref, v_ref, g_ref, y_ref, c_scr, *, n_ctx, n_all, hp):
    L = CHUNK
    row, col = _chunk_masks(L)
    incl = (col <= row, col >= row)
    csum_r = (incl[1].astype(BF16), incl[0].astype(BF16))
    eye = row == col
    ones_b = jnp.ones((L, LANES), BF16)
    scale = float(LANES) ** -0.5

    y_ref[...] = jnp.zeros(y_ref.shape, F32)
    c_scr[...] = jnp.zeros(c_scr.shape, F32)

    chains = [(h, d) for h in range(hp) for d in (0, 1)]
    lanes = lambda h: slice(h * LANES, (h + 1) * LANES)

    def body(i, carry):
        n0, m0 = carry
        r0 = (pl.multiple_of(i * L, L), pl.multiple_of(_bwd_chunk(i, n_ctx, n_all) * L, L))
        gt = [g_ref[0, :, pl.ds(r0[d], L)] for d in (0, 1)]
        lsr = [_log_sigmoid(g) for g in gt]
        bsum = [_mm_x01(lsr[d], csum_r[d]) for d in (0, 1)]
        row = lambda arr, h, d, kind: arr[d][h * GATE_ROWS + 2 * d + kind:h * GATE_ROWS + 2 * d + kind + 1, :]
        li_r = [row(gt, h, d, 0) for h, d in chains]
        b_r = [row(bsum, h, d, 1) for h, d in chains]
        q = [q_ref[0, pl.ds(r0[d], L), lanes(h)].astype(F32) * scale for h, d in chains]
        k = [k_ref[0, pl.ds(r0[d], L), lanes(h)].astype(F32) for h, d in chains]
        v = [v_ref[0, pl.ds(r0[d], L), lanes(h)].astype(F32) for h, d in chains]
        lf_r = [row(lsr, h, d, 1) for h, d in chains]
        b_c = [_mm_x01(jnp.where(incl[d], f, 0.0), ones_b) for (h, d), f in zip(chains, lf_r)]
        li_c = [_mm_x01(jnp.where(eye, x, 0.0), ones_b) for x in li_r]
        total = [b[:, L - 1:L] if d == 0 else b[:, 0:1] for (h, d), b in zip(chains, b_r)]
        c0 = [c_scr[2 * h + d] for h, d in chains]
        m_loc = [jnp.max(t - b + x, axis=1, keepdims=True) for t, b, x in zip(total, b_r, li_r)]
        kw = [kx * jnp.exp(t - b + x - m) for kx, t, b, x, m in zip(k, total, b_c, li_c, m_loc)]
        c_loc = [_mm_tn(a, vv) for a, vv in zip(kw, v)]
        m_new = [jnp.maximum(t + m, ml) for t, m, ml in zip(total, m0, m_loc)]
        a_old = [jnp.exp(t + m - mn) for t, m, mn in zip(total, m0, m_new)]
        a_loc = [jnp.exp(ml - mn) for ml, mn in zip(m_loc, m_new)]
        for (h, d), ao, c, al, cl in zip(chains, a_old, c0, a_loc, c_loc):
            c_scr[2 * h + d] = ao * c + al * cl
        n_new = [ao * n + al * jnp.sum(a, axis=0, keepdims=True)
                 for ao, n, al, a in zip(a_old, n0, a_loc, kw)]
        dlog = [jnp.where(incl[d], bc - br + x, -jnp.inf) for (h, d), bc, br, x in zip(chains, b_c, b_r, li_r)]
        inter = [bc + m for bc, m in zip(b_c, m0)]
        m_t = [jnp.maximum(x, jnp.max(dl, axis=1, keepdims=True)) for x, dl in zip(inter, dlog)]
        s = [_mm_nt(qq, kx) for qq, kx in zip(q, k)]
        p = [jnp.exp(dl - mt) * ss for dl, mt, ss in zip(dlog, m_t, s)]
        a_int = [jnp.exp(x - mt) for x, mt in zip(inter, m_t)]
        num = [ai * _mm(qq, c) + _mm(pp, vv) for ai, qq, c, pp, vv in zip(a_int, q, c0, p, v)]
        den = [ai * jnp.sum(qq * n, axis=1, keepdims=True) + jnp.sum(pp, axis=1, keepdims=True)
               for ai, qq, n, pp in zip(a_int, q, n0, p)]
        for (h, d), nu, de, mt in zip(chains, num, den, m_t):
            y_ref[0, pl.ds(r0[d], L), lanes(h)] += nu / jnp.maximum(jnp.abs(de), jnp.exp(-mt))
        return tuple(n_new), tuple(m_new)

    zn = tuple(jnp.zeros((1, LANES), F32) for _ in chains)
    zm = tuple(jnp.zeros((1, 1), F32) for _ in chains)
    lax.fori_loop(0, n_all, body, (zn, zm))


def _mlstm(qkvo, gates_t, *, n_ctx, n_all):
    bsz, s, _ = qkvo.shape
    h = MLSTM_HEADS
    hp = SCAN_HEADS_PER_STEP
    ng = h // hp
    wd = hp * LANES
    kern = functools.partial(_mlstm_kernel, n_ctx=n_ctx, n_all=n_all, hp=hp)
    blk = lambda off: pl.BlockSpec((1, s, wd), lambda i, j: (i, 0, j + off))
    return pl.pallas_call(
        kern, out_shape=jax.ShapeDtypeStruct((bsz, s, h * LANES), F32), grid=(bsz, ng),
        in_specs=[blk(0), blk(ng), blk(2 * ng),
                  pl.BlockSpec((1, hp * GATE_ROWS, s), lambda i, j: (i, j, 0))],
        out_specs=pl.BlockSpec((1, s, wd), lambda i, j: (i, 0, j)),
        scratch_shapes=[pltpu.VMEM((2 * hp, LANES, LANES), F32)],
        compiler_params=_cparams(("arbitrary", "arbitrary")), name="mlstm_scan",
    )(qkvo, qkvo, qkvo, gates_t)


def _gdn_kernel(q_ref, k_ref, v_ref, wq_ref, wk_ref, wv_ref, ab_ref, alog_ref, dtb_ref, y_ref,
                sh_scr, q_scr, k_scr, v_scr, s_scr, *, n_ctx, n_all, hp):
    L = CHUNK
    row, col = _chunk_masks(L)
    incl = (col <= row, col >= row)
    strict = (col < row, col > row)
    csum_r = (incl[1].astype(BF16), incl[0].astype(BF16))
    eye = row == col
    eye_f = jnp.where(eye, 1.0, 0.0)
    blk = tuple(jnp.right_shift(row, sh) == jnp.right_shift(col, sh) for sh in (4, 5, 6, 7))
    qscale = float(LANES) ** -0.5
    half = SHORT_K // 2
    edge = 16
    win_rows = 2 * L
    offsets = [o for o in range(-half, half + 1) if o != 0]
    wt = lax.broadcasted_iota(jnp.int32, (L, win_rows), 0)
    wu = lax.broadcasted_iota(jnp.int32, (L, win_rows), 1)
    for idx, o in enumerate(offsets):
        sh_scr[idx] = jnp.where(wu == wt + (edge + o), 1.0, 0.0).astype(BF16)
    gh = 2 if hp % 2 == 0 else 1
    gwid = gh * LANES
    zero_edge = jnp.zeros((edge, gwid), BF16)
    zero_fill = jnp.zeros((win_rows - L - 2 * edge, gwid), BF16)
    for src, w_ref, dst, norm, post in ((q_ref, wq_ref, q_scr, True, qscale), (k_ref, wk_ref, k_scr, True, 1.0),
                                        (v_ref, wv_ref, v_scr, False, 1.0)):
        for c in range(n_all):
            has_prev = c not in (0, n_ctx)
            has_next = c not in (n_ctx - 1, n_all - 1)
            for g0 in range(0, hp * LANES, gwid):
                ls = slice(g0, g0 + gwid)
                main = src[0, c * L:(c + 1) * L, ls]
                prev = src[0, c * L - edge:c * L, ls] if has_prev else zero_edge
                nxt = src[0, (c + 1) * L:(c + 1) * L + edge, ls] if has_next else zero_edge
                win = jnp.concatenate([prev, main, nxt, zero_fill], axis=0)
                acc = w_ref[half:half + 1, ls] * main.astype(F32)
                for idx, o in enumerate(offsets):
                    acc = acc + w_ref[o + half:o + half + 1, ls] * jnp.dot(
                        sh_scr[idx], win, preferred_element_type=F32)
                a = _silu(acc)
                for hh in range(gh):
                    ah = a[:, hh * LANES:(hh + 1) * LANES]
                    if norm:
                        ah = ah * lax.rsqrt(jnp.sum(ah * ah, axis=-1, keepdims=True) + EPS)
                    if post != 1.0:
                        ah = ah * post
                    dst[c * L:(c + 1) * L, g0 + hh * LANES:g0 + (hh + 1) * LANES] = ah.astype(dst.dtype)

    y_ref[...] = jnp.zeros(y_ref.shape, F32)
    s_scr[...] = jnp.zeros(s_scr.shape, F32)
    arate = jnp.exp(alog_ref[...])
    dtb = dtb_ref[...]

    chains = [(h, d) for h in range(hp) for d in (0, 1)]
    off_masks = [jnp.logical_and(blk[lvl], jnp.logical_not(blk[lvl - 1])) for lvl in range(1, len(blk))]

    def body(i, carry):
        r0 = (pl.multiple_of(i * L, L), pl.multiple_of(_bwd_chunk(i, n_ctx, n_all) * L, L))
        ab = [ab_ref[0, :, pl.ds(r0[d], L)] for d in (0, 1)]
        g8 = [-arate * _softplus(a + dtb) for a in ab]
        beta8 = [_sigmoid(a) for a in ab]
        gcs = [_mm_x01(g8[d], csum_r[d]) for d in (0, 1)]
        row = lambda arr, h, d, kind: arr[d][h * GATE_ROWS + 2 * d + kind:h * GATE_ROWS + 2 * d + kind + 1, :]
        gc_r = [row(gcs, h, d, 0) for h, d in chains]
        lanes = lambda h: slice(h * LANES, (h + 1) * LANES)
        q = [q_scr[pl.ds(r0[d], L), lanes(h)].astype(F32) for h, d in chains]
        k = [k_scr[pl.ds(r0[d], L), lanes(h)].astype(F32) for h, d in chains]
        v = [v_scr[pl.ds(r0[d], L), lanes(h)].astype(F32) for h, d in chains]
        cols_t = [_rows_to_columns(gcs[d], beta8[d]) for d in (0, 1)]
        nrow = hp * GATE_ROWS
        gc_c = [_column(cols_t[d], h * GATE_ROWS + 2 * d) for h, d in chains]
        beta_c = [_column(cols_t[d], nrow + h * GATE_ROWS + 2 * d + 1) for h, d in chains]
        total = [g[:, L - 1:L] if d == 0 else g[:, 0:1] for (h, d), g in zip(chains, gc_r)]
        dec = [jnp.exp(jnp.where(incl[d], c - r, -jnp.inf)) for (h, d), c, r in zip(chains, gc_c, gc_r)]
        kk = [_mm_nt(x, x) for x in k]
        a_mat = [jnp.where(strict[d], b * m * e, 0.0) for (h, d), b, m, e in zip(chains, beta_c, kk, dec)]
        egc = [jnp.exp(c) for c in gc_c]
        rhs = [jnp.concatenate([b * vv, (b * e) * kx], axis=1)
               for b, vv, e, kx in zip(beta_c, v, egc, k)]
        pw = [-jnp.where(blk[0], a, 0.0) for a in a_mat]
        t_inv = [eye_f + p for p in pw]
        for _ in range(3):
            pw = [_mm(p, p) for p in pw]
            t_inv = [t + _mm(t, p) for t, p in zip(t_inv, pw)]
        for om in off_masks:
            ta = [_mm(t, jnp.where(om, a, 0.0)) for t, a in zip(t_inv, a_mat)]
            t_inv = [t - _mm(x, t) for t, x in zip(t_inv, ta)]
        sol = [_mm(t, x) for t, x in zip(t_inv, rhs)]
        s0 = [s_scr[2 * h + d] for h, d in chains]
        v_new = [x[:, :LANES] - _mm(x[:, LANES:], s) for x, s in zip(sol, s0)]
        kdec = [jnp.exp(t - c) * kx for t, c, kx in zip(total, gc_c, k)]
        upd = [_mm_tn(kd, vn) for kd, vn in zip(kdec, v_new)]
        for (h, d), t, s, u in zip(chains, total, s0, upd):
            s_scr[2 * h + d] = jnp.exp(t) * s + u
        qk = [_mm_nt(qq, kx) * e for qq, kx, e in zip(q, k, dec)]
        o = [_mm(e * qq, s) + _mm(m, vn) for e, qq, s, m, vn in zip(egc, q, s0, qk, v_new)]
        for (h, d), oo in zip(chains, o):
            y_ref[0, pl.ds(r0[d], L), lanes(h)] += oo
        return carry

    lax.fori_loop(0, n_all, body, 0)


def _gdn(qkv, short_w, ab_t, alog_rows, dtb_rows, *, n_ctx, n_all):
    bsz, s, _ = qkv.shape
    h = GDN_HEADS
    hp = GDN_HEADS_PER_STEP
    ng = h // hp
    wd = hp * LANES
    kern = functools.partial(_gdn_kernel, n_ctx=n_ctx, n_all=n_all, hp=hp)
    blk = lambda off: pl.BlockSpec((1, s, wd), lambda i, j: (i, 0, j + off), pipeline_mode=pl.Buffered(1))
    wblk = lambda off: pl.BlockSpec((SHORT_K, wd), lambda i, j: (0, j + off))
    return pl.pallas_call(
        kern, out_shape=jax.ShapeDtypeStruct((bsz, s, h * LANES), F32), grid=(bsz, ng),
        in_specs=[blk(0), blk(ng), blk(2 * ng), wblk(0), wblk(ng), wblk(2 * ng),
                  pl.BlockSpec((1, hp * GATE_ROWS, s), lambda i, j: (i, j, 0)),
                  pl.BlockSpec((hp * GATE_ROWS, 1), lambda i, j: (j, 0)),
                  pl.BlockSpec((hp * GATE_ROWS, 1), lambda i, j: (j, 0))],
        out_specs=pl.BlockSpec((1, s, wd), lambda i, j: (i, 0, j), pipeline_mode=pl.Buffered(1)),
        scratch_shapes=[pltpu.VMEM((SHORT_K - 1, CHUNK, 2 * CHUNK), BF16), pltpu.VMEM((s, wd), BF16),
                        pltpu.VMEM((s, wd), BF16), pltpu.VMEM((s, wd), BF16),
                        pltpu.VMEM((2 * hp, LANES, LANES), F32)],
        compiler_params=_cparams(("arbitrary", "arbitrary")), name="gdn_scan",
    )(qkv, qkv, qkv, short_w, short_w, short_w, ab_t, alog_rows, dtb_rows)


def _conformer_kernel(glu_ref, w_ref, b_ref, lg_ref, lb_ref, o_ref, u_scr, *, sc, s_len):
    ch = o_ref.shape[-1]
    pad = 16
    rt = 32
    half = CONF_K // 2
    zeros_pad = jnp.zeros((pad, ch), F32)
    u_scr[0:pad, :] = zeros_pad
    u_scr[pad + sc:2 * pad + sc, :] = zeros_pad
    u_scr[2 * pad + s_len:3 * pad + s_len, :] = zeros_pad

    def fill(i, carry):
        r0 = pl.multiple_of(i * rt, rt)
        off = jnp.where(r0 < sc, pad, 2 * pad)
        a = glu_ref[0, pl.ds(r0, rt), 0:ch].astype(F32)
        gt = glu_ref[0, pl.ds(r0, rt), ch:2 * ch].astype(F32)
        u_scr[pl.ds(pl.multiple_of(r0 + off, 8), rt), :] = a * _sigmoid(gt)
        return carry

    lax.fori_loop(0, s_len // rt, fill, 0)

    def conv(i, carry):
        r0 = pl.multiple_of(i * rt, rt)
        off = jnp.where(r0 < sc, pad, 2 * pad)
        nwin = rt + 2 * pad
        start = pl.multiple_of(r0 + off - pad, 8)
        parts = []
        for l0 in range(0, ch, 2 * LANES):
            ls = slice(l0, l0 + 2 * LANES)
            win = u_scr[pl.ds(start, nwin), ls]
            acc = jnp.zeros((rt, 2 * LANES), F32)
            for res in range(8):
                shifted = win if res == 0 else pltpu.roll(win, nwin - res, axis=0)
                for a8 in range(0, nwin - rt + 1, 8):
                    j = a8 + res - (pad - half)
                    if 0 <= j < CONF_K:
                        acc = acc + w_ref[j:j + 1, ls] * shifted[a8:a8 + rt, :]
            parts.append(acc)
        y = jnp.concatenate(parts, axis=1) + b_ref[...]
        mu = jnp.mean(y, axis=-1, keepdims=True)
        yc = y - mu
        var = jnp.mean(yc * yc, axis=-1, keepdims=True)
        z = yc * lax.rsqrt(var + EPS) * lg_ref[...] + lb_ref[...]
        o_ref[0, pl.ds(r0, rt), :] = _silu(z).astype(o_ref.dtype)
        return carry

    lax.fori_loop(0, s_len // rt, conv, 0, unroll=4)


def _conformer(glu, dw, dw_b, ln_g, ln_b, *, sc):
    bsz, s, c2 = glu.shape
    ch = c2 // 2
    kern = functools.partial(_conformer_kernel, sc=sc, s_len=s)
    vec = lambda: pl.BlockSpec((1, ch), lambda i: (0, 0))
    return pl.pallas_call(
        kern, out_shape=jax.ShapeDtypeStruct((bsz, s, ch), BF16), grid=(bsz,),
        in_specs=[pl.BlockSpec((1, s, c2), lambda i: (i, 0, 0)),
                  pl.BlockSpec((CONF_K, ch), lambda i: (0, 0)), vec(), vec(), vec()],
        out_specs=pl.BlockSpec((1, s, ch), lambda i: (i, 0, 0)),
        scratch_shapes=[pltpu.VMEM((s + 48, ch), F32)],
        compiler_params=_cparams(("arbitrary",)), name="conformer_conv",
    )(glu, dw, dw_b.reshape(1, ch), ln_g.reshape(1, ch), ln_b.reshape(1, ch))


def _finish_kernel(y_ref, gate_ref, *rest, n_heads, even, mod_idx, up_mod_idx, tn):
    if even:
        c_ref, x_ref, mod_ref, hg_ref, w_ref, ng_ref, wu_ref, o_ref, og_ref, ov_ref = rest
    else:
        x_ref, mod_ref, hg_ref, w_ref, ng_ref, wu_ref, o_ref, og_ref, ov_ref = rest
    d = x_ref.shape[-1]
    y = y_ref[0]
    parts = []
    for h in range(n_heads):
        yh = y[:, h * LANES:(h + 1) * LANES]
        ms = jnp.mean(yh * yh, axis=-1, keepdims=True)
        parts.append(yh * lax.rsqrt(ms + EPS))
    yn = jnp.concatenate(parts, axis=1)
    gate = gate_ref[0].astype(F32)
    if even:
        m = yn * hg_ref[...] * _sigmoid(gate)
        cat = jnp.concatenate([m.astype(BF16), c_ref[0].astype(BF16)], axis=1)
    else:
        cat = ((yn * hg_ref[...]) * _silu(gate)).astype(BF16)
    out = jnp.dot(cat, w_ref[...], preferred_element_type=F32)
    mod = mod_ref[0, 0]
    x1 = x_ref[0] + mod[:, mod_idx * d:(mod_idx + 1) * d] * out
    o_ref[0] = x1
    ms = jnp.mean(x1 * x1, axis=-1, keepdims=True)
    r = x1 * lax.rsqrt(ms + EPS)
    shift = mod[:, up_mod_idx * d:(up_mod_idx + 1) * d]
    scale = mod[:, (up_mod_idx + 1) * d:(up_mod_idx + 2) * d]
    hb = (r * ng_ref[...] * (1.0 + scale) + shift).astype(BF16)
    fdim = og_ref.shape[-1]
    for up_ref, off in ((og_ref, 0), (ov_ref, fdim)):
        for n0 in range(0, fdim, tn):
            nw = min(tn, fdim - n0)
            acc = jnp.dot(hb, wu_ref[:, off + n0:off + n0 + nw], preferred_element_type=F32)
            up_ref[0, :, n0:n0 + nw] = acc.astype(up_ref.dtype)


def _finish(y, gate_src, gate_blk, c_out, xs, mod, head_g, w_out, norm2_g, w_up, *, n_heads, tile0=0):
    bsz, s, d = xs.shape
    tm = TOKEN_TILE
    wdt = n_heads * LANES
    fdim = w_up.shape[1] // 2
    even = c_out is not None
    tok = lambda width, blk=0: pl.BlockSpec((1, tm, width), lambda i, j: (i, j + tile0, blk))
    in_specs = [tok(wdt), tok(wdt, gate_blk)]
    args = [y, gate_src]
    if even:
        in_specs.append(tok(c_out.shape[-1]))
        args.append(c_out)
    in_specs += [tok(d),
                 pl.BlockSpec((1, 1, 1, mod.shape[-1]), lambda i, j: (i, jnp.minimum(j + tile0, 1), 0, 0)),
                 pl.BlockSpec((1, wdt), lambda i, j: (0, 0)),
                 pl.BlockSpec(w_out.shape, lambda i, j: (0, 0)),
                 pl.BlockSpec((1, d), lambda i, j: (0, 0)),
                 pl.BlockSpec(w_up.shape, lambda i, j: (0, 0))]
    args += [xs, mod, head_g.reshape(1, wdt), w_out, norm2_g.reshape(1, d), w_up]
    kern = functools.partial(_finish_kernel, n_heads=n_heads, even=even, mod_idx=2, up_mod_idx=3, tn=512)
    return pl.pallas_call(
        kern, grid=(bsz, s // tm - tile0), in_specs=in_specs,
        out_shape=[jax.ShapeDtypeStruct((bsz, s, d), F32), jax.ShapeDtypeStruct((bsz, s, fdim), BF16),
                   jax.ShapeDtypeStruct((bsz, s, fdim), BF16)],
        out_specs=[tok(d), tok(fdim), tok(fdim)],
        compiler_params=_cparams(("arbitrary", "arbitrary")), name="mixer_out_ffn_up",
    )(*args)


def _ffn_down_kernel(gm_ref, gp_ref, gn_ref, val_ref, x_ref, mod_ref, dw_ref, dwb_ref, wd_ref, *rest,
                     tile0, n_tiles_all, final, mod_idx):
    if final:
        fg_ref, o_ref, ext_scr, left_scr, right_scr, act_scr, sh_small, sh_big = rest
    else:
        o_ref, ext_scr, left_scr, right_scr, act_scr, sh_small, sh_big = rest
    tm = TOKEN_TILE
    gw = GRID_W
    f = gm_ref.shape[-1]
    d = x_ref.shape[-1]
    j = pl.program_id(1) + tile0
    lat = jnp.where(j > 0, 1.0, 0.0)
    jrow = jnp.broadcast_to(j, (gw, 1))
    has_prev = jrow >= 2
    has_next = jnp.logical_and(jrow >= 1, jrow <= n_tiles_all - 2)
    shifts = {}
    for n in (gw, tm):
        t = lax.broadcasted_iota(jnp.int32, (n, n), 0)
        u = lax.broadcasted_iota(jnp.int32, (n, n), 1)
        col = jnp.bitwise_and(t, gw - 1)
        ctx_seq = jnp.broadcast_to(j, (n, n)) == 0
        take_left = jnp.logical_and(u == t - 1, jnp.logical_or(col != 0, ctx_seq))
        take_right = jnp.logical_and(u == t + 1, jnp.logical_or(col != gw - 1, ctx_seq))
        shifts[n] = sh_small if n == gw else sh_big
        shifts[n][0] = jnp.where(take_left, 1.0, 0.0).astype(BF16)
        shifts[n][1] = jnp.where(take_right, 1.0, 0.0).astype(BF16)

    for n0 in range(0, f, FFN_CW):
        cols = slice(n0, n0 + FFN_CW)
        blocks = ((0, gw, jnp.where(has_prev, gp_ref[0, :, cols], 0.0).astype(BF16)),
                  (gw, tm, gm_ref[0, :, cols]),
                  (gw + tm, gw, jnp.where(has_next, gn_ref[0, :, cols], 0.0).astype(BF16)))
        slot = (n0 // FFN_CW) % 2
        for r0, n, g in blocks:
            ext_scr[slot, r0:r0 + n, :] = g.astype(F32)
            left_scr[slot, r0:r0 + n, :] = jnp.dot(shifts[n][0], g, preferred_element_type=F32)
            right_scr[slot, r0:r0 + n, :] = jnp.dot(shifts[n][1], g, preferred_element_type=F32)
        for rs in range(tm // gw):
            acc = jnp.zeros((gw, FFN_CW), F32)
            for dr in range(3):
                base = (rs + dr) * gw
                for dc, src in enumerate((left_scr, ext_scr, right_scr)):
                    wv = dw_ref[dr * 3 + dc:dr * 3 + dc + 1, cols]
                    if dr != 1:
                        wv = wv * lat
                    acc = acc + src[slot, base:base + gw, :] * wv
            gate = acc + dwb_ref[:, cols]
            a = _silu(gate) * val_ref[0, rs * gw:(rs + 1) * gw, cols]
            act_scr[rs * gw:(rs + 1) * gw, cols] = a.astype(BF16)

    out = jnp.dot(act_scr[...], wd_ref[...], preferred_element_type=F32)
    g2 = mod_ref[0, 0][:, mod_idx * d:(mod_idx + 1) * d]
    x2 = x_ref[0] + g2 * out
    if final:
        ms = jnp.mean(x2 * x2, axis=-1, keepdims=True)
        x2 = x2 * lax.rsqrt(ms + EPS) * fg_ref[...]
    o_ref[0] = x2


def _ffn_down(gate, val, xs, mod, dw9, dw_b, w_down, *, tile0=0, final_g=None):
    bsz, s, f = gate.shape
    d = xs.shape[-1]
    tm = TOKEN_TILE
    gw = GRID_W
    n_all = s // tm
    per = tm // gw
    n_rows = s // gw
    final = final_g is not None
    tok = lambda width: pl.BlockSpec((1, tm, width), lambda i, j: (i, j + tile0, 0))
    in_specs = [tok(f),
                pl.BlockSpec((1, gw, f), lambda i, j: (i, jnp.maximum((j + tile0) * per - 1, tile0 * per), 0)),
                pl.BlockSpec((1, gw, f), lambda i, j: (i, jnp.minimum((j + tile0 + 1) * per, n_rows - 1), 0)),
                tok(f), tok(d),
                pl.BlockSpec((1, 1, 1, mod.shape[-1]), lambda i, j: (i, jnp.minimum(j + tile0, 1), 0, 0)),
                pl.BlockSpec((9, f), lambda i, j: (0, 0)),
                pl.BlockSpec((1, f), lambda i, j: (0, 0)),
                pl.BlockSpec(w_down.shape, lambda i, j: (0, 0))]
    args = [gate, gate, gate, val, xs, mod, dw9, dw_b.reshape(1, f), w_down]
    if final:
        in_specs.append(pl.BlockSpec((1, d), lambda i, j: (0, 0)))
        args.append(final_g.reshape(1, d))
        out_shape = jax.ShapeDtypeStruct((bsz, s - tile0 * tm, d), F32)
        out_spec = pl.BlockSpec((1, tm, d), lambda i, j: (i, j, 0))
    else:
        out_shape = jax.ShapeDtypeStruct((bsz, s, d), F32)
        out_spec = tok(d)
    kern = functools.partial(_ffn_down_kernel, tile0=tile0, n_tiles_all=n_all, final=final, mod_idx=5)
    return pl.pallas_call(
        kern, out_shape=out_shape, grid=(bsz, n_all - tile0), in_specs=in_specs, out_specs=out_spec,
        scratch_shapes=[pltpu.VMEM((2, tm + 2 * gw, FFN_CW), F32)] * 3
        + [pltpu.VMEM((tm, f), BF16), pltpu.VMEM((2, gw, gw), BF16), pltpu.VMEM((2, tm, tm), BF16)],
        compiler_params=_cparams(("arbitrary", "arbitrary")), name="convglu_down",
    )(*args)


def _head_major_rows(w_cols, n_heads):
    d = w_cols.shape[0]
    w = w_cols.reshape(d, 4, n_heads).transpose(2, 1, 0)
    w = jnp.concatenate([w, jnp.zeros((n_heads, GATE_ROWS - 4, d), w.dtype)], axis=1)
    return w.reshape(n_heads * GATE_ROWS, d)


def _head_major_vec(b_cols, n_heads):
    b = b_cols.reshape(4, n_heads).T
    b = jnp.concatenate([b, jnp.zeros((n_heads, GATE_ROWS - 4), b.dtype)], axis=1)
    return b.reshape(n_heads * GATE_ROWS)


def kernel(x, c, ctx, c_ctx, ada_w, ada_b, norm1_g, norm2_g, e_w_in, e_b_in, e_head_g, e_conf_dw, e_conf_dw_b, e_conf_ln_g, e_conf_ln_b, e_w_out, o_w_in, o_short_w, o_a_log, o_dt_bias, o_head_g, o_w_out, f_w_up, f_dw, f_dw_b, f_w_down, final_g):
    bsz, t, d = x.shape
    sc = ctx.shape[1]
    depth = ada_w.shape[0]
    assert sc == TOKEN_TILE and t % TOKEN_TILE == 0 and t % CHUNK == 0 and sc % CHUNK == 0
    s = sc + t
    n_ctx, n_all = sc // CHUNK, s // CHUNK
    mw = MLSTM_HEADS * LANES
    gwd = GDN_HEADS * LANES
    ffn = f_w_down.shape[1]

    xs = jnp.concatenate([ctx, x], axis=1)

    rows = 16
    c_rows = jnp.concatenate([c, c_ctx[None, :], jnp.zeros((rows - bsz - 1, d), F32)], axis=0)
    mod_all = _ada_mod(c_rows, ada_w, ada_b)

    for layer in range(depth):
        last = layer == depth - 1
        tile0 = 1 if last else 0
        ml = mod_all[layer]
        mod = jnp.stack([jnp.broadcast_to(ml[bsz][None], (bsz, 6 * d)), ml[:bsz]], axis=1)[:, :, None, :]
        jj = layer // 2
        w_up = f_w_up[layer].astype(BF16)
        if layer % 2 == 0:
            w_in, b_in = e_w_in[jj], e_b_in[jj]
            g0 = 4 * mw
            g1 = g0 + 4 * MLSTM_HEADS
            w_main = jnp.concatenate([w_in[:, :g0], w_in[:, g1:]], axis=1).astype(BF16)
            b_main = jnp.concatenate([b_in[:g0], b_in[g1:]])
            w_rows = _head_major_rows(w_in[:, g0:g1], MLSTM_HEADS).astype(BF16)
            b_rows = _head_major_vec(b_in[g0:g1], MLSTM_HEADS)
            qkvo, glu, gates_t = _proj(xs, mod, norm1_g[layer], w_main, b_main, mod_idx=0,
                                       seg_widths=(g0, w_in.shape[1] - g1), w_rows=w_rows, b_rows=b_rows)
            y = _mlstm(qkvo, gates_t, n_ctx=n_ctx, n_all=n_all)
            c_out = _conformer(glu, e_conf_dw[jj], e_conf_dw_b[jj], e_conf_ln_g[jj], e_conf_ln_b[jj], sc=sc)
            x1, gate, val = _finish(y, qkvo, 3, c_out, xs, mod, e_head_g[jj], e_w_out[jj].astype(BF16),
                                    norm2_g[layer], w_up, n_heads=MLSTM_HEADS, tile0=tile0)
        else:
            w_in = o_w_in[jj]
            g0 = 4 * gwd
            w_main = w_in[:, :g0].astype(BF16)
            w_rows = _head_major_rows(w_in[:, g0:], GDN_HEADS).astype(BF16)
            b_rows = jnp.zeros((GDN_HEADS * GATE_ROWS,), F32)
            qkv, z, ab_t = _proj(xs, mod, norm1_g[layer], w_main, jnp.zeros((g0,), F32), mod_idx=0,
                                 seg_widths=(3 * gwd, gwd), w_rows=w_rows, b_rows=b_rows)
            zpad = jnp.zeros((GDN_HEADS, GATE_ROWS - 3), F32)
            alog_rows = jnp.concatenate([o_a_log[jj][0][:, None], jnp.zeros((GDN_HEADS, 1), F32),
                                         o_a_log[jj][1][:, None], zpad], axis=1).reshape(-1, 1)
            dtb_rows = jnp.concatenate([o_dt_bias[jj][0][:, None], jnp.zeros((GDN_HEADS, 1), F32),
                                        o_dt_bias[jj][1][:, None], zpad], axis=1).reshape(-1, 1)
            y = _gdn(qkv, o_short_w[jj], ab_t, alog_rows, dtb_rows, n_ctx=n_ctx, n_all=n_all)
            x1, gate, val = _finish(y, z, 0, None, xs, mod, jnp.tile(o_head_g[jj], GDN_HEADS),
                                    o_w_out[jj].astype(BF16), norm2_g[layer], w_up, n_heads=GDN_HEADS, tile0=tile0)
        xs = _ffn_down(gate, val, x1, mod, f_dw[layer].reshape(9, ffn), f_dw_b[layer],
                       f_w_down[layer].astype(BF16), tile0=tile0, final_g=final_g if last else None)
    return xs
```
